```python
import jax, jax.numpy as jnp
from jax import lax
import numpy as np

D_MODEL = 2048
BATCH = 2
SEQ = 4096
DEPTH = 4
DEC_BATCH = 8
DEC_SEQ = 8
PAST_LEN = 16384
PAGE_SIZE = 128

HEAD_DIM = 128
N_HEADS_TOTAL = D_MODEL // HEAD_DIM
GLA_HEADS = N_HEADS_TOTAL // 4
NSA_HEADS = (N_HEADS_TOTAL - GLA_HEADS) // 2
MOBA_HEADS = N_HEADS_TOTAL - GLA_HEADS - NSA_HEADS
NSA_KV = 2
MOBA_KV = 2
CMP_LEN = 32
CMP_STRIDE = 16
CMP_HID = 128
SLC_BLK = 64
SLC_TOPN = 16
WINDOW = 512
MOBA_BLK = 256
MOBA_TOPK = 3
GLA_DV = HEAD_DIM
GLA_DK = HEAD_DIM // 2
GLA_RANK = 16
GLA_TAU = 16.0
GLA_CHUNK = 64
D_FF = 4 * D_MODEL
ROPE_THETA = 10000.0
EPS = 1e-6
SCALE = HEAD_DIM ** -0.5
SPARSE_QBLK = 32
WIN_QBLK = 128

kernel_name = 'hymba_nsa_moba_gla_step'


def _proj_sizes():
    hk = NSA_KV * HEAD_DIM
    return (NSA_HEADS * HEAD_DIM, hk, hk, hk, hk, hk, hk, 3 * NSA_HEADS,
            MOBA_HEADS * HEAD_DIM, MOBA_KV * HEAD_DIM, MOBA_KV * HEAD_DIM,
            GLA_HEADS * GLA_DK, GLA_HEADS * GLA_DK, GLA_HEADS * GLA_DV, GLA_RANK, GLA_HEADS * GLA_DV)


def _split(z):
    idx = np.cumsum(_proj_sizes())[:-1].tolist()
    return jnp.split(z, idx, axis=-1)


def _rms(x, g):
    xf = x.astype(jnp.float32)
    y = xf * lax.rsqrt(jnp.mean(xf * xf, axis=-1, keepdims=True) + EPS)
    return (y * g.astype(jnp.float32)).astype(x.dtype)


def _rope(x, pos):
    half = HEAD_DIM // 2
    inv = ROPE_THETA ** (-jnp.arange(half, dtype=jnp.float32) / half)
    ang = (pos.astype(jnp.float32)[:, None] * inv)[:, None, :]
    cos, sin = jnp.cos(ang), jnp.sin(ang)
    xf = x.astype(jnp.float32)
    x1, x2 = xf[..., :half], xf[..., half:]
    return jnp.concatenate([x1 * cos - x2 * sin, x2 * cos + x1 * sin], axis=-1).astype(x.dtype)


def _masked_softmax(s, mask):
    s = jnp.where(mask, s.astype(jnp.float32), -1e30)
    p = jnp.where(mask, jnp.exp(s - jnp.max(s, axis=-1, keepdims=True)), 0.0)
    return p / jnp.maximum(jnp.sum(p, axis=-1, keepdims=True), 1e-30)


def _group(a, g):
    B, T, H, D = a.shape
    return a.transpose(0, 2, 1, 3).reshape(B, g, H // g, T, D)


def _ungroup(o):
    B, G, R, T, D = o.shape
    return o.reshape(B, G * R, T, D).transpose(0, 2, 1, 3)


def _to_blocks(rows, blk):
    B, L, G, D = rows.shape
    nb = -(-L // blk)
    r = jnp.pad(rows, ((0, 0), (0, nb * blk - L), (0, 0), (0, 0)))
    return r.reshape(B, nb, blk, G, D).transpose(0, 3, 1, 2, 4)


def _blocked(fn, qs, pos, qblk):
    B, G, R, T, D = qs[0].shape
    n = T // qblk
    split = lambda a: jnp.moveaxis(a.reshape(B, G, R, n, qblk, a.shape[-1]), 3, 0)
    outs = lax.map(lambda a: fn(*a[0], a[1]), (tuple(split(q) for q in qs), pos.reshape(n, qblk)))
    return tuple(jnp.moveaxis(o, 0, 3).reshape(B, G, R, T, o.shape[-1]) for o in outs)


def _compress(rows, pe, w1, b1, w2):
    B, L, G, D = rows.shape
    per = CMP_LEN // CMP_STRIDE
    nch = L // CMP_STRIDE
    nc = nch - per + 1
    ch = rows[:, :nch * CMP_STRIDE].reshape(B, nch, CMP_STRIDE, G, D)
    w1r = w1.reshape(per, CMP_STRIDE, D, CMP_HID)
    per_pe = pe.reshape(per, CMP_STRIDE, 1, D)
    hid = b1
    for j in range(per):
        hid = hid + jnp.einsum('bcsgd,sdh->bcgh', ch[:, j:j + nc] + per_pe[j], w1r[j])
    return jnp.einsum('bcgh,hd->bgcd', jax.nn.gelu(hid), w2)


def _nsa_cs(q_c, q_s, q_pos, kc, vc, ks, vs):
    B, G, R, Q, D = q_c.shape
    C, NS = kc.shape[2], ks.shape[2]
    cmp_end = jnp.arange(C) * CMP_STRIDE + CMP_LEN - 1
    p_c = _masked_softmax(jnp.einsum('bgrqd,bgcd->bgrqc', q_c, kc) * SCALE, cmp_end[None, :] <= q_pos[:, None])
    o_c = jnp.einsum('bgrqc,bgcd->bgrqd', p_c.astype(vc.dtype), vc)
    ci, sj = jnp.arange(C)[:, None], jnp.arange(NS)[None, :]
    ov = ((ci * CMP_STRIDE < (sj + 1) * SLC_BLK) & (ci * CMP_STRIDE + CMP_LEN > sj * SLC_BLK)).astype(jnp.float32)
    imp = jnp.einsum('bgrqc,cj->bgqj', p_c, ov)
    cur = q_pos // SLC_BLK
    jb = jnp.arange(NS)[None, :]
    forced = (jb == 0) | (jb == cur[:, None]) | (jb == cur[:, None] - 1)
    score = jnp.where(forced, jnp.inf, jnp.where(jb <= cur[:, None], imp, -jnp.inf))
    n_sel = min(SLC_TOPN, NS)
    _, idx = lax.top_k(score, n_sel)
    bi = jnp.arange(B)[:, None, None, None]
    gi = jnp.arange(G)[None, :, None, None]
    k_sel = ks[bi, gi, idx].reshape(B, G, Q, n_sel * SLC_BLK, D)
    v_sel = vs[bi, gi, idx].reshape(B, G, Q, n_sel * SLC_BLK, D)
    kpos = idx[..., None] * SLC_BLK + jnp.arange(SLC_BLK)
    m_s = (kpos <= q_pos[:, None, None]).reshape(B, G, Q, n_sel * SLC_BLK)
    p_s = _masked_softmax(jnp.einsum('bgrqd,bgqkd->bgrqk', q_s, k_sel) * SCALE, m_s[:, :, None])
    o_s = jnp.einsum('bgrqk,bgqkd->bgrqd', p_s.astype(v_sel.dtype), v_sel)
    return o_c, o_s


def _moba(q, q_pos, kb, vb, kmean):
    B, G, R, Q, D = q.shape
    NB = kb.shape[2]
    own = q_pos // MOBA_BLK
    s_g = jnp.einsum('bgrqd,bgnd->bgrqn', q.astype(jnp.float32), kmean)
    past = jnp.arange(NB)[None, :] < own[:, None]
    s_g = jnp.where(past, s_g, -jnp.inf)
    k_top = min(MOBA_TOPK, NB)
    _, idx = lax.top_k(s_g, k_top)
    ok = idx < own[:, None]
    idx_all = jnp.concatenate([idx, jnp.broadcast_to(own[:, None], idx.shape[:-1] + (1,))], axis=-1)
    ok_all = jnp.concatenate([ok, jnp.ones(idx.shape[:-1] + (1,), bool)], axis=-1)
    bi = jnp.arange(B)[:, None, None, None, None]
    gi = jnp.arange(G)[None, :, None, None, None]
    nk = (k_top + 1) * MOBA_BLK
    k_sel = kb[bi, gi, idx_all].reshape(B, G, R, Q, nk, D)
    v_sel = vb[bi, gi, idx_all].reshape(B, G, R, Q, nk, D)
    kpos = idx_all[..., None] * MOBA_BLK + jnp.arange(MOBA_BLK)
    m = (ok_all[..., None] & (kpos <= q_pos[:, None, None])).reshape(B, G, R, Q, nk)
    p = _masked_softmax(jnp.einsum('bgrqd,bgrqkd->bgrqk', q, k_sel) * SCALE, m)
    return (jnp.einsum('bgrqk,bgrqkd->bgrqd', p.astype(v_sel.dtype), v_sel),)


def _window_prompt(q, k, v):
    B, G, R, S, D = q.shape
    nb = S // WIN_QBLK
    nback = WINDOW // WIN_QBLK
    pad = nback * WIN_QBLK
    def bands(a):
        ap = jnp.pad(a, ((0, 0), (0, 0), (pad, 0), (0, 0))).reshape(B, G, nb + nback, WIN_QBLK, D)
        return jnp.concatenate([ap[:, :, i:i + nb] for i in range(nback + 1)], axis=3)
    kb, vb = bands(k), bands(v)
    qb = q.reshape(B, G, R, nb, WIN_QBLK, D)
    blk0 = jnp.arange(nb)[:, None] * WIN_QBLK
    qpos = blk0 + jnp.arange(WIN_QBLK)
    kpos = blk0 - pad + jnp.arange((nback + 1) * WIN_QBLK)
    d = qpos[:, :, None] - kpos[:, None, :]
    m = (d >= 0) & (d < WINDOW) & (kpos[:, None, :] >= 0)
    p = _masked_softmax(jnp.einsum('bgrnqd,bgnkd->bgrnqk', qb, kb) * SCALE, m)
    o = jnp.einsum('bgrnqk,bgnkd->bgrnqd', p.astype(vb.dtype), vb)
    return o.reshape(B, G, R, S, D)


def _attend_dense(q, k, v, m):
    p = _masked_softmax(jnp.einsum('bgrqd,bgkd->bgrqk', q, k) * SCALE, m)
    return jnp.einsum('bgrqk,bgkd->bgrqd', p.astype(v.dtype), v)


def _gla_chunk(S0, q, k, v, g):
    qf, kf, vf = q.astype(jnp.float32), k.astype(jnp.float32), v.astype(jnp.float32)
    b = jnp.cumsum(g, axis=2)
    C = q.shape[2]
    causal = jnp.tril(jnp.ones((C, C), bool))
    diff = jnp.where(causal[:, :, None], b[:, :, :, None, :] - b[:, :, None, :, :], -jnp.inf)
    A = jnp.einsum('bhid,bhjd,bhijd->bhij', qf, kf, jnp.exp(diff))
    o = jnp.einsum('bhcd,bhde->bhce', qf * jnp.exp(b), S0) + jnp.einsum('bhij,bhje->bhie', A, vf)
    b_last = b[:, :, -1]
    S = jnp.exp(b_last)[..., None] * S0 + jnp.einsum('bhcd,bhce->bhde', kf * jnp.exp(b_last[:, :, None] - b), vf)
    return S, o


def _gla(q, k, v, g, S0, chunk):
    B, H, T, _ = q.shape
    n = T // chunk
    to_chunks = lambda a: jnp.moveaxis(a.reshape(B, H, n, chunk, a.shape[-1]), 2, 0)
    S, o = lax.scan(lambda S, inp: _gla_chunk(S, *inp), S0, tuple(to_chunks(a) for a in (q, k, v, g)))
    return jnp.moveaxis(o, 0, 2).reshape(B, H, T, -1), S


def _layer(x, c, pos, lw, past):
    (w_ada, b_ada, n1, n2, w_in, nsa_g, cpos, cw1, cb1, cw2, moba_g, gw2, gb, gng, w_out, w_up, w_down) = lw
    B, T, _ = x.shape
    mod = jax.nn.silu(c) @ w_ada + b_ada
    sh1, sc1, gt1, sh2, sc2, gt2 = jnp.split(mod[:, None, :], 6, axis=-1)
    h = _rms(x, n1) * (1 + sc1) + sh1
    (nq, nkc, nvc, nks, nvs, nkw, nvw, ngt, mq, mk, mv, gq, gk, gv, ga, gr) = _split(h @ w_in)
    hd = lambda a, n: a.reshape(B, T, n, -1)
    q_a = _rms(hd(nq, NSA_HEADS), nsa_g[0])
    q_a_rot = _rope(q_a, pos)
    ks_new = _rope(_rms(hd(nks, NSA_KV), nsa_g[2]), pos)
    kw_new = _rope(_rms(hd(nkw, NSA_KV), nsa_g[3]), pos)
    vw_new = hd(nvw, NSA_KV)
    nsa_rows = jnp.stack([hd(nkc, NSA_KV), hd(nvc, NSA_KV), ks_new, hd(nvs, NSA_KV)], axis=2)
    win_rows = jnp.stack([kw_new, vw_new], axis=2)
    q_b = _rope(_rms(hd(mq, MOBA_HEADS), moba_g[0]), pos)
    k_b = _rope(_rms(hd(mk, MOBA_KV), moba_g[1]), pos)
    moba_rows = jnp.stack([k_b, hd(mv, MOBA_KV)], axis=2)
    if past is None:
        nsa_ctx, moba_ctx = nsa_rows, moba_rows
        S0 = jnp.zeros((B, GLA_HEADS, GLA_DK, GLA_DV), jnp.float32)
    else:
        nsa_ctx = jnp.concatenate([past['nsa'].astype(x.dtype), nsa_rows], axis=1)
        moba_ctx = jnp.concatenate([past['moba'].astype(x.dtype), moba_rows], axis=1)
        S0 = past['gla'].astype(jnp.float32)
    qblk = SPARSE_QBLK if T % SPARSE_QBLK == 0 else T
    kc_cmp = _rms(_compress(nsa_ctx[:, :, 0], cpos[0], cw1[0], cb1[0], cw2[0]), nsa_g[1])
    vc_cmp = _compress(nsa_ctx[:, :, 1], cpos[1], cw1[1], cb1[1], cw2[1])
    ks_blk = _to_blocks(nsa_ctx[:, :, 2], SLC_BLK)
    vs_blk = _to_blocks(nsa_ctx[:, :, 3], SLC_BLK)
    q_sg = _group(q_a_rot, NSA_KV)
    o_c, o_s = _blocked(lambda a, b, p: _nsa_cs(a, b, p, kc_cmp, vc_cmp, ks_blk, vs_blk),
                        (_group(q_a, NSA_KV), q_sg), pos, qblk)
    if past is None:
        o_w = _window_prompt(q_sg, kw_new.transpose(0, 2, 1, 3), vw_new.transpose(0, 2, 1, 3))
        win_state = win_rows[:, -min(WINDOW, T):]
    else:
        win_ctx = jnp.concatenate([past['win'].astype(x.dtype), win_rows], axis=1)
        wb = past['win'].shape[1]
        kpos = past['past_len'] - wb + jnp.arange(wb + T)
        d = pos[:, None] - kpos[None, :]
        o_w = _attend_dense(q_sg, win_ctx[:, :, 0].transpose(0, 2, 1, 3), win_ctx[:, :, 1].transpose(0, 2, 1, 3),
                            (d >= 0) & (d < WINDOW))
        win_state = win_ctx[:, -min(WINDOW, past['past_len'] + T):]
    gate = jax.nn.sigmoid(ngt.astype(jnp.float32)).reshape(B, T, NSA_HEADS, 3)
    o_a = gate[..., 0:1] * _ungroup(o_c) + gate[..., 1:2] * _ungroup(o_s) + gate[..., 2:3] * _ungroup(o_w)
    kb_blk = _to_blocks(moba_ctx[:, :, 0], MOBA_BLK)
    vb_blk = _to_blocks(moba_ctx[:, :, 1], MOBA_BLK)
    kmean = jnp.mean(kb_blk.astype(jnp.float32), axis=3)
    o_b = _blocked(lambda a, p: _moba(a, p, kb_blk, vb_blk, kmean), (_group(q_b, MOBA_KV),), pos, qblk)[0]
    tr = lambda a: a.transpose(0, 2, 1, 3)
    q_g = tr(hd(gq, GLA_HEADS)) * (GLA_DK ** -0.5)
    k_g = tr(hd(gk, GLA_HEADS))
    v_g = tr(hd(gv, GLA_HEADS))
    lg = tr((jax.nn.log_sigmoid((ga @ gw2 + gb).astype(jnp.float32)) / GLA_TAU).reshape(B, T, GLA_HEADS, GLA_DK))
    o_g, S_fin = _gla(q_g, k_g, v_g, lg, S0, GLA_CHUNK if T % GLA_CHUNK == 0 else T)
    o_g = _rms(tr(o_g).astype(x.dtype), gng).reshape(B, T, -1) * jax.nn.silu(gr)
    mix = jnp.concatenate([o_a.reshape(B, T, -1).astype(x.dtype), _ungroup(o_b).reshape(B, T, -1).astype(x.dtype),
                           o_g.astype(x.dtype)], axis=-1)
    x = x + gt1 * (mix @ w_out)
    h2 = _rms(x, n2) * (1 + sc2) + sh2
    x = x + gt2 * (jnp.square(jax.nn.relu(h2 @ w_up)) @ w_down)
    return x, nsa_rows, win_state, moba_rows, S_fin.astype(x.dtype)


def setup_inputs(seed: int = 0) -> dict:
    key = jax.random.key(seed)
    ks = iter(jax.random.split(key, 40))
    nrm = lambda shape, s=1.0: s * jax.random.normal(next(ks), shape, jnp.float32)
    D = D_MODEL
    n_pages = PAST_LEN // PAGE_SIZE
    n_pool = (DEC_BATCH * n_pages * 5) // 4
    w_buf = min(WINDOW, PAST_LEN)
    n_in = sum(_proj_sizes())
    perm = jax.random.permutation(next(ks), n_pool)
    page_table = perm[:DEC_BATCH * n_pages].reshape(DEC_BATCH, n_pages).astype(jnp.int32)
    return {
        'x_prompt': nrm((BATCH, SEQ, D)),
        'x_sample': nrm((DEC_BATCH, DEC_SEQ, D)),
        'cache_nsa': nrm((DEPTH, n_pool, PAGE_SIZE, 4, NSA_KV, HEAD_DIM)),
        'cache_moba': nrm((DEPTH, n_pool, PAGE_SIZE, 2, MOBA_KV, HEAD_DIM)),
        'state_nsa_win': nrm((DEPTH, DEC_BATCH, w_buf, 2, NSA_KV, HEAD_DIM)),
        'state_gla': nrm((DEPTH, DEC_BATCH, GLA_HEADS, GLA_DK, GLA_DV)),
        'page_table': page_table,
        'c_prompt': nrm((BATCH, D)),
        'c_sample': nrm((DEC_BATCH, D)),
        'w_ada': nrm((DEPTH, D, 6 * D), D ** -0.5),
        'b_ada': nrm((DEPTH, 6 * D), 0.02),
        'norm1_g': 1.0 + nrm((DEPTH, D), 0.05),
        'norm2_g': 1.0 + nrm((DEPTH, D), 0.05),
        'w_in': nrm((DEPTH, D, n_in), D ** -0.5),
        'nsa_qk_g': 1.0 + nrm((DEPTH, 4, HEAD_DIM), 0.05),
        'nsa_cmp_pos': nrm((DEPTH, 2, CMP_LEN, HEAD_DIM), 0.1),
        'nsa_cmp_w1': nrm((DEPTH, 2, CMP_LEN * HEAD_DIM, CMP_HID), (CMP_LEN * HEAD_DIM) ** -0.5),
        'nsa_cmp_b1': nrm((DEPTH, 2, CMP_HID), 0.02),
        'nsa_cmp_w2': nrm((DEPTH, 2, CMP_HID, HEAD_DIM), CMP_HID ** -0.5),
        'moba_qk_g': 1.0 + nrm((DEPTH, 2, HEAD_DIM), 0.05),
        'gla_w_a2': nrm((DEPTH, GLA_RANK, GLA_HEADS * GLA_DK), GLA_RANK ** -0.5),
        'gla_b_a': nrm((DEPTH, GLA_HEADS * GLA_DK), 0.1),
        'gla_norm_g': 1.0 + nrm((DEPTH, GLA_DV), 0.05),
        'w_out': nrm((DEPTH, D, D), D ** -0.5),
        'w_up': nrm((DEPTH, D, D_FF), D ** -0.5),
        'w_down': nrm((DEPTH, D_FF, D), D_FF ** -0.5),
    }


def reference(x_prompt, x_sample, cache_nsa, cache_moba, state_nsa_win, state_gla, page_table, c_prompt, c_sample,
              w_ada, b_ada, norm1_g, norm2_g, w_in, nsa_qk_g, nsa_cmp_pos, nsa_cmp_w1, nsa_cmp_b1, nsa_cmp_w2,
              moba_qk_g, gla_w_a2, gla_b_a, gla_norm_g, w_out, w_up, w_down):
    db, t_dec = x_sample.shape[0], x_sample.shape[1]
    past_len = page_table.shape[1] * cache_nsa.shape[2]
    pos_p = jnp.arange(x_prompt.shape[1], dtype=jnp.int32)
    pos_s = past_len + jnp.arange(t_dec, dtype=jnp.int32)

    def gather(pool):
        return pool[page_table].reshape((db, past_len) + pool.shape[2:])

    weights = (w_ada, b_ada, norm1_g, norm2_g, w_in, nsa_qk_g, nsa_cmp_pos, nsa_cmp_w1, nsa_cmp_b1, nsa_cmp_w2,
               moba_qk_g, gla_w_a2, gla_b_a, gla_norm_g, w_out, w_up, w_down)
    yp, ys = x_prompt, x_sample
    nsa_p, nsa_s, win_p, win_s, moba_p, moba_s, gla_p, gla_s = [], [], [], [], [], [], [], []
    for l in range(DEPTH):
        lw = tuple(w[l] for w in weights)
        yp, a, b, c, d = _layer(yp, c_prompt, pos_p, lw, None)
        past = {'nsa': gather(cache_nsa[l]), 'moba': gather(cache_moba[l]), 'win': state_nsa_win[l],
                'gla': state_gla[l], 'past_len': past_len}
        ys, e, f, g, h = _layer(ys, c_sample, pos_s, lw, past)
        nsa_p.append(a); win_p.append(b); moba_p.append(c); gla_p.append(d)
        nsa_s.append(e); win_s.append(f); moba_s.append(g); gla_s.append(h)
    return (yp, ys, jnp.stack(nsa_p), jnp.stack(nsa_s), jnp.stack(win_p), jnp.stack(win_s),
            jnp.stack(moba_p), jnp.stack(moba_s), jnp.stack(gla_p), jnp.stack(gla_s))
```

```python
import functools

import numpy as np
import jax
import jax.numpy as jnp
from jax import lax
from jax.experimental import pallas as pl
from jax.experimental.pallas import tpu as pltpu

F32 = jnp.float32
BF16 = jnp.bfloat16
HI = lax.Precision.HIGHEST

D_MODEL = 2048
HEAD_DIM = 128
NSA_HEADS = 6
NSA_KV = 2
NSA_REP = NSA_HEADS // NSA_KV
MOBA_HEADS = 6
MOBA_KV = 2
MOBA_REP = MOBA_HEADS // MOBA_KV
GLA_HEADS = 4
GLA_DK = 64
GLA_DV = 128
GLA_RANK = 16
GLA_TAU = 16.0
GLA_CHUNK = 64
CMP_LEN = 32
CMP_STRIDE = 16
CMP_HID = 128
SLC_BLK = 64
SLC_TOPN = 16
WINDOW = 512
MOBA_BLK = 256
MOBA_TOPK = 3
D_FF = 4 * D_MODEL
ROPE_THETA = 10000.0
EPS = 1e-6
SCALE = HEAD_DIM ** -0.5
NEG = -1e30

ZT = 256
C_QA = 0
C_QAR = 768
C_NSA = 1536
C_WIN = 2560
C_QB = 3072
C_MOBA = 3840
C_GQ = 4352
C_GV = 4608
C_GR = 5120
C_GK = 5632
C_MISC = 5888
NZ = 6144
_ZT_NORM = (0, 1, 2)
_ZT_ROPE = (3, 4, 5, 8, 10, 12, 13, 14, 15)

VMEM_LIMIT = 56 * 1024 * 1024

NT_DIMS = (((1,), (1,)), ((), ()))
TN_DIMS = (((0,), (0,)), ((), ()))


def _cparams(sem):
    return pltpu.CompilerParams(dimension_semantics=sem, vmem_limit_bytes=VMEM_LIMIT)


def _ada_kernel(c_ref, w_ref, b_ref, o_ref):
    c = c_ref[...]
    a = c * jax.nn.sigmoid(c)
    o_ref[...] = jnp.dot(a, w_ref[...], precision=HI, preferred_element_type=F32) + b_ref[...]


def _ada(c_all, w_ada, b_ada):
    L, D, N = w_ada.shape
    R = c_all.shape[0]
    tn = 512
    return pl.pallas_call(
        _ada_kernel,
        grid=(L, N // tn),
        in_specs=[pl.BlockSpec((R, D), lambda l, j: (0, 0)),
                  pl.BlockSpec((None, D, tn), lambda l, j: (l, 0, j)),
                  pl.BlockSpec((None, 1, tn), lambda l, j: (l, 0, j))],
        out_specs=pl.BlockSpec((None, R, tn), lambda l, j: (l, 0, j)),
        out_shape=jax.ShapeDtypeStruct((L, R, N), F32),
        compiler_params=_cparams(("parallel", "parallel")),
        name="ada_mod",
    )(c_all, w_ada, b_ada.reshape(L, 1, N))


def _mod_norm(x_ref, sc_ref, sh_ref, gn_ref, h_ref):
    x = x_ref[...]
    y = x * lax.rsqrt(jnp.mean(x * x, axis=-1, keepdims=True) + EPS) * gn_ref[...]
    h_ref[...] = (y * (1.0 + sc_ref[...]) + sh_ref[...]).astype(BF16)


def _head_rms(a, g):
    return a * lax.rsqrt(jnp.mean(a * a, axis=-1, keepdims=True) + EPS) * g


def _nmm_zz_kernel(x_ref, sc_ref, sh_ref, gn_ref, w_ref, gz_ref, cos_ref, sin_ref, o_ref, h_ref):
    j = pl.program_id(1)

    @pl.when(j == 0)
    def _():
        _mod_norm(x_ref, sc_ref, sh_ref, gn_ref, h_ref)

    acc = jnp.dot(h_ref[...], w_ref[...], preferred_element_type=F32)
    is_norm = j < 3
    is_rope = ((j >= 3) & (j <= 5)) | (j == 8) | (j == 10) | ((j >= 12) & (j <= 15))

    @pl.when(jnp.logical_not(is_norm | is_rope))
    def _():
        o_ref[...] = acc

    @pl.when(is_norm)
    def _():
        for hh in range(ZT // HEAD_DIM):
            sl = slice(hh * HEAD_DIM, (hh + 1) * HEAD_DIM)
            o_ref[:, sl] = _head_rms(acc[:, sl], gz_ref[:, sl])

    @pl.when(is_rope)
    def _():
        for hh in range(ZT // HEAD_DIM):
            sl = slice(hh * HEAD_DIM, (hh + 1) * HEAD_DIM)
            y = _head_rms(acc[:, sl], gz_ref[:, sl])
            o_ref[:, sl] = y * cos_ref[...] + pltpu.roll(y, HEAD_DIM // 2, 1) * sin_ref[...]


def _nmm_relu2_kernel(x_ref, sc_ref, sh_ref, gn_ref, w_ref, o_ref, h_ref):
    @pl.when(pl.program_id(1) == 0)
    def _():
        _mod_norm(x_ref, sc_ref, sh_ref, gn_ref, h_ref)

    acc = jnp.dot(h_ref[...], w_ref[...], preferred_element_type=F32)
    r = jnp.maximum(acc, 0.0)
    o_ref[...] = (r * r).astype(o_ref.dtype)


def _mod_specs(sc, tm, tpb):
    rows = sc.shape[1]
    D = sc.shape[2]
    return pl.BlockSpec((None, rows, D), lambda i, j: (i // tpb, 0, 0))


def _nmm_zz(x, sc, sh, gn, wz, l, gz, cos, sin, tm, tpb):
    M, D = x.shape
    tn = ZT
    return pl.pallas_call(
        _nmm_zz_kernel,
        grid=(M // tm, NZ // tn),
        in_specs=[pl.BlockSpec((tm, D), lambda i, j: (i, 0)),
                  _mod_specs(sc, tm, tpb), _mod_specs(sh, tm, tpb),
                  pl.BlockSpec((None, 1, D), lambda i, j: (l, 0, 0)),
                  pl.BlockSpec((None, D, tn), lambda i, j: (l, 0, j)),
                  pl.BlockSpec((None, 1, tn), lambda i, j: (l, 0, j)),
                  pl.BlockSpec((tm, HEAD_DIM), lambda i, j: (i % tpb, 0)),
                  pl.BlockSpec((tm, HEAD_DIM), lambda i, j: (i % tpb, 0))],
        out_specs=pl.BlockSpec((tm, tn), lambda i, j: (i, j)),
        out_shape=jax.ShapeDtypeStruct((M, NZ), F32),
        scratch_shapes=[pltpu.VMEM((tm, D), BF16)],
        compiler_params=_cparams(("parallel", "arbitrary")),
        name="in_proj",
    )(x, sc, sh, gn, wz, gz, cos, sin)


def _nmm_relu2(x, sc, sh, gn, wu, l, tm, tpb):
    M, D = x.shape
    N = wu.shape[2]
    tn = 512
    return pl.pallas_call(
        _nmm_relu2_kernel,
        grid=(M // tm, N // tn),
        in_specs=[pl.BlockSpec((tm, D), lambda i, j: (i, 0)),
                  _mod_specs(sc, tm, tpb), _mod_specs(sh, tm, tpb),
                  pl.BlockSpec((None, 1, D), lambda i, j: (l, 0, 0)),
                  pl.BlockSpec((None, D, tn), lambda i, j: (l, 0, j))],
        out_specs=pl.BlockSpec((tm, tn), lambda i, j: (i, j)),
        out_shape=jax.ShapeDtypeStruct((M, N), BF16),
        scratch_shapes=[pltpu.VMEM((tm, D), BF16)],
        compiler_params=_cparams(("parallel", "arbitrary")),
        name="mlp_up",
    )(x, sc, sh, gn, wu)


def _outproj_kernel(oa_ref, ob_ref, og_ref, w_ref, x_ref, gt_ref, o_ref):
    na, nb = oa_ref.shape[1], ob_ref.shape[1]
    acc = jnp.dot(oa_ref[...], w_ref[0:na, :], preferred_element_type=F32)
    acc += jnp.dot(ob_ref[...], w_ref[na:na + nb, :], preferred_element_type=F32)
    acc += jnp.dot(og_ref[...], w_ref[na + nb:, :], preferred_element_type=F32)
    o_ref[...] = x_ref[...] + gt_ref[...] * acc


def _outproj(oa, ob, og, wo, l, x, gt, tm, tpb):
    M, D = x.shape
    tn = 512
    rows = gt.shape[1]
    return pl.pallas_call(
        _outproj_kernel,
        grid=(M // tm, D // tn),
        in_specs=[pl.BlockSpec((tm, oa.shape[1]), lambda i, j: (i, 0)),
                  pl.BlockSpec((tm, ob.shape[1]), lambda i, j: (i, 0)),
                  pl.BlockSpec((tm, og.shape[1]), lambda i, j: (i, 0)),
                  pl.BlockSpec((None, D, tn), lambda i, j: (l, 0, j)),
                  pl.BlockSpec((tm, tn), lambda i, j: (i, j)),
                  pl.BlockSpec((None, rows, tn), lambda i, j: (i // tpb, 0, j))],
        out_specs=pl.BlockSpec((tm, tn), lambda i, j: (i, j)),
        out_shape=jax.ShapeDtypeStruct((M, D), F32),
        compiler_params=_cparams(("parallel", "parallel")),
        name="out_proj",
    )(oa, ob, og, wo, x, gt)


def _down_kernel(u_ref, w_ref, x_ref, gt_ref, o_ref, acc_ref):
    k = pl.program_id(2)

    @pl.when(k == 0)
    def _():
        acc_ref[...] = jnp.zeros_like(acc_ref)

    acc_ref[...] += jnp.dot(u_ref[...], w_ref[...], preferred_element_type=F32)

    @pl.when(k == pl.num_programs(2) - 1)
    def _():
        o_ref[...] = x_ref[...] + gt_ref[...] * acc_ref[...]


def _down(u, wd, l, x, gt, tm, tpb):
    M, D = x.shape
    K = u.shape[1]
    tn, tk = 512, 2048
    rows = gt.shape[1]
    return pl.pallas_call(
        _down_kernel,
        grid=(M // tm, D // tn, K // tk),
        in_specs=[pl.BlockSpec((tm, tk), lambda i, j, k: (i, k)),
                  pl.BlockSpec((None, tk, tn), lambda i, j, k: (l, k, j)),
                  pl.BlockSpec((tm, tn), lambda i, j, k: (i, j)),
                  pl.BlockSpec((None, rows, tn), lambda i, j, k: (i // tpb, 0, j))],
        out_specs=pl.BlockSpec((tm, tn), lambda i, j, k: (i, j)),
        out_shape=jax.ShapeDtypeStruct((M, D), F32),
        scratch_shapes=[pltpu.VMEM((tm, tn), F32)],
        compiler_params=_cparams(("parallel", "parallel", "arbitrary")),
        name="mlp_down",
    )(u, wd, x, gt)


def _compress_kernel(rows_ref, pe_ref, w1_ref, b1_ref, w2_ref, g_ref, o_ref, *, nch):
    slot = pl.program_id(1)
    pa = jnp.zeros((nch, CMP_HID), F32)
    pb = jnp.zeros((nch, CMP_HID), F32)
    for s in range(CMP_STRIDE):
        xs = rows_ref[pl.ds(s, nch, stride=CMP_STRIDE), :]
        wa = w1_ref[s * HEAD_DIM:(s + 1) * HEAD_DIM, :]
        wb = w1_ref[(CMP_STRIDE + s) * HEAD_DIM:(CMP_STRIDE + s + 1) * HEAD_DIM, :]
        pa += jnp.dot(xs + pe_ref[s:s + 1, :], wa, precision=HI, preferred_element_type=F32)
        pb += jnp.dot(xs + pe_ref[CMP_STRIDE + s:CMP_STRIDE + s + 1, :], wb, precision=HI,
                      preferred_element_type=F32)
    hid = b1_ref[...] + pa + pltpu.roll(pb, nch - 1, 0)
    tok = jnp.dot(jax.nn.gelu(hid), w2_ref[...], precision=HI, preferred_element_type=F32)

    @pl.when(slot == 0)
    def _():
        o_ref[...] = _head_rms(tok, g_ref[...])

    @pl.when(slot != 0)
    def _():
        o_ref[...] = tok


def _compress_prompt(zz3, pe, w1, b1, w2, gk, l):
    B, T, _ = zz3.shape
    nch = T // CMP_STRIDE
    cb = C_NSA // HEAD_DIM
    return pl.pallas_call(
        functools.partial(_compress_kernel, nch=nch),
        grid=(B, 2, NSA_KV),
        in_specs=[pl.BlockSpec((None, T, HEAD_DIM), lambda b, s, g: (b, 0, cb + 2 * s + g)),
                  pl.BlockSpec((None, None, CMP_LEN, HEAD_DIM), lambda b, s, g: (l, s, 0, 0)),
                  pl.BlockSpec((None, None, CMP_LEN * HEAD_DIM, CMP_HID), lambda b, s, g: (l, s, 0, 0)),
                  pl.BlockSpec((None, None, 1, CMP_HID), lambda b, s, g: (l, s, 0, 0)),
                  pl.BlockSpec((None, None, CMP_HID, HEAD_DIM), lambda b, s, g: (l, s, 0, 0)),
                  pl.BlockSpec((None, None, 1, HEAD_DIM), lambda b, s, g: (l, 1, 0, 0))],
        out_specs=pl.BlockSpec((None, None, None, nch, HEAD_DIM), lambda b, s, g: (s, b, g, 0, 0)),
        out_shape=jax.ShapeDtypeStruct((2, B, NSA_KV, nch, HEAD_DIM), F32),
        compiler_params=_cparams(("parallel", "parallel", "parallel")),
        name="nsa_compress",
    )(zz3, pe, w1, b1, w2, gk)


def _rank_desc(score, n):
    idx = lax.broadcasted_iota(jnp.int32, (1, n), 1)
    rank = jnp.zeros(score.shape, jnp.int32)
    for i in range(n):
        ci = score[:, i:i + 1]
        beats = (ci > score) | ((ci == score) & (idx > i))
        rank = rank + beats.astype(jnp.int32)
    return rank


def _online_update(carry, s, msk, v):
    m, l, acc = carry
    s = jnp.where(msk, s, NEG)
    m_new = jnp.maximum(m, jnp.max(s, axis=-1, keepdims=True))
    alpha = jnp.exp(m - m_new)
    p = jnp.where(msk, jnp.exp(s - m_new), 0.0)
    l = alpha * l + jnp.sum(p, axis=-1, keepdims=True)
    acc = alpha * acc + jnp.dot(p.astype(BF16), v, preferred_element_type=F32)
    return m_new, l, acc


def _online_init(rows):
    return (jnp.full((rows, 1), NEG, F32), jnp.zeros((rows, 1), F32), jnp.zeros((rows, HEAD_DIM), F32))


def _online_finish(carry):
    _, l, acc = carry
    return acc / jnp.maximum(l, 1e-30)


def _nsa_prompt_kernel(qa_ref, qar_ref, kc_ref, vc_ref, ks_ref, vs_ref, kw_ref, vw_ref, gt_ref, ov_ref, o_ref,
                       *, tq, n_sel):
    qi = pl.program_id(2)
    R = NSA_REP
    C = kc_ref.shape[0]
    NS = ov_ref.shape[1]
    tk = tq
    q0 = qi * tq
    qpos = q0 + lax.broadcasted_iota(jnp.int32, (tq, 1), 0)

    kc = kc_ref[...]
    vc = vc_ref[...].astype(BF16)
    cend = lax.broadcasted_iota(jnp.int32, (1, C), 1) * CMP_STRIDE + (CMP_LEN - 1)
    cmask = cend <= qpos
    imp = jnp.zeros((tq, NS), F32)
    o_c = []
    for r in range(R):
        q = qa_ref[:, r * HEAD_DIM:(r + 1) * HEAD_DIM]
        s = lax.dot_general(q, kc, NT_DIMS, precision=HI, preferred_element_type=F32) * SCALE
        s = jnp.where(cmask, s, NEG)
        p = jnp.where(cmask, jnp.exp(s - jnp.max(s, axis=-1, keepdims=True)), 0.0)
        p = p / jnp.maximum(jnp.sum(p, axis=-1, keepdims=True), 1e-30)
        o_c.append(jnp.dot(p.astype(BF16), vc, preferred_element_type=F32))
        imp = imp + jnp.dot(p, ov_ref[...], precision=HI, preferred_element_type=F32)

    jb = lax.broadcasted_iota(jnp.int32, (1, NS), 1)
    cur = qpos // SLC_BLK
    forced = (jb == 0) | (jb == cur) | (jb == cur - 1)
    score = jnp.where(forced, jnp.inf, jnp.where(jb <= cur, imp, -jnp.inf))
    sel = (_rank_desc(score, NS) < n_sel).astype(BF16)

    q3 = jnp.concatenate([qar_ref[:, r * HEAD_DIM:(r + 1) * HEAD_DIM] for r in range(R)], axis=0).astype(BF16)

    def sel_body(kt, carry):
        k0 = pl.multiple_of(kt * tk, tk)
        k = ks_ref[pl.ds(k0, tk), :].astype(BF16)
        v = vs_ref[pl.ds(k0, tk), :].astype(BF16)
        s = lax.dot_general(q3, k, NT_DIMS, preferred_element_type=F32) * SCALE
        kpos_e = k0 + lax.broadcasted_iota(jnp.int32, (NS, tk), 1)
        expand = (lax.broadcasted_iota(jnp.int32, (NS, tk), 0) == kpos_e // SLC_BLK).astype(BF16)
        selx = jnp.dot(sel, expand, preferred_element_type=F32)
        kpos = k0 + lax.broadcasted_iota(jnp.int32, (1, tk), 1)
        msk = (selx > 0.5) & (kpos <= qpos)
        return _online_update(carry, s, jnp.concatenate([msk] * R, axis=0), v)

    o_s = _online_finish(lax.fori_loop(0, qi + 1, sel_body, _online_init(R * tq)))

    carry = _online_init(R * tq)
    for w in range(WINDOW // tk + 1):
        kt = qi - (WINDOW // tk) + w
        k0 = pl.multiple_of(jnp.maximum(kt, 0) * tk, tk)
        k = kw_ref[pl.ds(k0, tk), :].astype(BF16)
        v = vw_ref[pl.ds(k0, tk), :].astype(BF16)
        s = lax.dot_general(q3, k, NT_DIMS, preferred_element_type=F32) * SCALE
        d = qpos - (k0 + lax.broadcasted_iota(jnp.int32, (1, tk), 1))
        msk = (d >= 0) & (d < WINDOW) & (kt >= 0)
        carry = _online_update(carry, s, jnp.concatenate([msk] * R, axis=0), v)
    o_w = _online_finish(carry)

    gate = jax.nn.sigmoid(gt_ref[...])
    for r in range(R):
        rows = slice(r * tq, (r + 1) * tq)
        o = (gate[:, 3 * r:3 * r + 1] * o_c[r] + gate[:, 3 * r + 1:3 * r + 2] * o_s[rows]
             + gate[:, 3 * r + 2:3 * r + 3] * o_w[rows])
        o_ref[:, r * HEAD_DIM:(r + 1) * HEAD_DIM] = o.astype(o_ref.dtype)


def _overlap_matrix(C, NS):
    ci, sj = np.arange(C)[:, None], np.arange(NS)[None, :]
    return ((ci * CMP_STRIDE < (sj + 1) * SLC_BLK) & (ci * CMP_STRIDE + CMP_LEN > sj * SLC_BLK)).astype(np.float32)


def _nsa_prompt(zz3, kcvc):
    B, T, _ = zz3.shape
    tq = 256
    C = kcvc.shape[3]
    NS = -(-T // SLC_BLK)
    ov = jnp.asarray(_overlap_matrix(C, NS))
    gw = NSA_REP * HEAD_DIM
    col = lambda c: c // HEAD_DIM
    kv_spec = lambda c: pl.BlockSpec((None, T, HEAD_DIM), lambda b, g, i: (b, 0, col(c) + g))
    return pl.pallas_call(
        functools.partial(_nsa_prompt_kernel, tq=tq, n_sel=min(SLC_TOPN, NS)),
        grid=(B, NSA_KV, T // tq),
        in_specs=[pl.BlockSpec((None, tq, gw), lambda b, g, i: (b, i, C_QA // gw + g)),
                  pl.BlockSpec((None, tq, gw), lambda b, g, i: (b, i, C_QAR // gw + g)),
                  pl.BlockSpec((None, None, None, C, HEAD_DIM), lambda b, g, i: (0, b, g, 0, 0)),
                  pl.BlockSpec((None, None, None, C, HEAD_DIM), lambda b, g, i: (1, b, g, 0, 0)),
                  kv_spec(C_NSA + 2 * ZT), kv_spec(C_NSA + 3 * ZT), kv_spec(C_WIN), kv_spec(C_WIN + ZT),
                  pl.BlockSpec((None, tq, HEAD_DIM), lambda b, g, i: (b, i, col(C_MISC) + g)),
                  pl.BlockSpec((C, NS), lambda b, g, i: (0, 0))],
        out_specs=pl.BlockSpec((None, tq, gw), lambda b, g, i: (b, i, g)),
        out_shape=jax.ShapeDtypeStruct((B, T, NSA_HEADS * HEAD_DIM), BF16),
        compiler_params=_cparams(("parallel", "parallel", "arbitrary")),
        name="nsa_prompt",
    )(zz3, zz3, kcvc, kcvc, zz3, zz3, zz3, zz3, zz3, ov)


def _moba_prompt_kernel(q_ref, kb_ref, vb_ref, o_ref, kmean_ref, *, tq, k_top):
    qi = pl.program_id(2)
    R = MOBA_REP
    NB = kmean_ref.shape[0]
    tk = MOBA_BLK
    own = qi
    qpos = qi * tq + lax.broadcasted_iota(jnp.int32, (tq, 1), 0)

    @pl.when(qi == 0)
    def _():
        kmean_ref[...] = jnp.sum(kb_ref[...].reshape(NB, MOBA_BLK, HEAD_DIM), axis=1) * (1.0 / MOBA_BLK)

    nbi = lax.broadcasted_iota(jnp.int32, (1, NB), 1)
    past = nbi < own
    sels = []
    for r in range(R):
        q = q_ref[:, r * HEAD_DIM:(r + 1) * HEAD_DIM]
        sg = lax.dot_general(q, kmean_ref[...], NT_DIMS, precision=HI, preferred_element_type=F32)
        sg = jnp.where(past, sg, -jnp.inf)
        sels.append((((_rank_desc(sg, NB) < k_top) & past) | (nbi == own)).astype(F32))
    sel3 = jnp.concatenate(sels, axis=0)
    q3 = jnp.concatenate([q_ref[:, r * HEAD_DIM:(r + 1) * HEAD_DIM] for r in range(R)], axis=0).astype(BF16)
    qpos3 = jnp.concatenate([qpos] * R, axis=0)

    def body(n, carry):
        k0 = pl.multiple_of(n * tk, tk)
        k = kb_ref[pl.ds(k0, tk), :].astype(BF16)
        v = vb_ref[pl.ds(k0, tk), :].astype(BF16)
        s = lax.dot_general(q3, k, NT_DIMS, preferred_element_type=F32) * SCALE
        chosen = jnp.sum(jnp.where(nbi == n, sel3, 0.0), axis=-1, keepdims=True) > 0.5
        kpos = k0 + lax.broadcasted_iota(jnp.int32, (1, tk), 1)
        return _online_update(carry, s, chosen & (kpos <= qpos3), v)

    o = _online_finish(lax.fori_loop(0, qi + 1, body, _online_init(R * tq)))
    for r in range(R):
        o_ref[:, r * HEAD_DIM:(r + 1) * HEAD_DIM] = o[r * tq:(r + 1) * tq].astype(o_ref.dtype)


def _moba_prompt(zz3):
    B, T, _ = zz3.shape
    tq = MOBA_BLK
    NB = T // MOBA_BLK
    gw = MOBA_REP * HEAD_DIM
    col = lambda c: c // HEAD_DIM
    return pl.pallas_call(
        functools.partial(_moba_prompt_kernel, tq=tq, k_top=min(MOBA_TOPK, NB)),
        grid=(B, MOBA_KV, T // tq),
        in_specs=[pl.BlockSpec((None, tq, gw), lambda b, g, i: (b, i, C_QB // gw + g)),
                  pl.BlockSpec((None, T, HEAD_DIM), lambda b, g, i: (b, 0, col(C_MOBA) + g)),
                  pl.BlockSpec((None, T, HEAD_DIM), lambda b, g, i: (b, 0, col(C_MOBA + ZT) + g))],
        out_specs=pl.BlockSpec((None, tq, gw), lambda b, g, i: (b, i, g)),
        out_shape=jax.ShapeDtypeStruct((B, T, MOBA_HEADS * HEAD_DIM), BF16),
        scratch_shapes=[pltpu.VMEM((NB, HEAD_DIM), F32)],
        compiler_params=_cparams(("parallel", "parallel", "arbitrary")),
        name="moba_prompt",
    )(zz3, zz3, zz3)


def _gla_kernel(q_ref, k_ref, v_ref, r_ref, misc_ref, gw2_ref, gb_ref, gng_ref, s0_ref, og_ref, s_ref,
                lg_ref, st_ref, *, chunk):
    cg = pl.program_id(1)
    tt = q_ref.shape[0]

    @pl.when(cg == 0)
    def _():
        for h in range(GLA_HEADS):
            st_ref[h] = s0_ref[h].T

    ga = misc_ref[:, 16:16 + GLA_RANK]
    pre = jnp.dot(ga, gw2_ref[...], precision=HI, preferred_element_type=F32) + gb_ref[...]
    lg_ref[...] = jax.nn.log_sigmoid(pre) * (1.0 / GLA_TAU)

    ri = lax.broadcasted_iota(jnp.int32, (chunk, chunk), 0)
    ci = lax.broadcasted_iota(jnp.int32, (chunk, chunk), 1)
    causal = ci <= ri
    tri = causal.astype(F32)
    mid = chunk // 2

    def body(c, _):
        r0 = pl.multiple_of(c * chunk, chunk)
        rows = pl.ds(r0, chunk)
        for h in range(GLA_HEADS):
            ks = slice(h * GLA_DK, (h + 1) * GLA_DK)
            vs = slice(h * GLA_DV, (h + 1) * GLA_DV)
            b = jnp.dot(tri, lg_ref[rows, ks], precision=HI, preferred_element_type=F32)
            bm = b[mid:mid + 1, :]
            bl = b[chunk - 1:chunk, :]
            q = q_ref[rows, ks] * (GLA_DK ** -0.5)
            k = k_ref[rows, ks]
            v = v_ref[rows, vs]
            a = lax.dot_general(q * jnp.exp(b - bm), k * jnp.exp(bm - b), NT_DIMS, precision=HI,
                                preferred_element_type=F32)
            a = jnp.where(causal, a, 0.0)
            st = st_ref[h]
            o = lax.dot_general(q * jnp.exp(b), st, NT_DIMS, precision=HI, preferred_element_type=F32)
            o = o + jnp.dot(a, v, precision=HI, preferred_element_type=F32)
            kd = k * jnp.exp(bl - b)
            st_ref[h] = st * jnp.exp(bl) + lax.dot_general(v, kd, TN_DIMS, precision=HI,
                                                           preferred_element_type=F32)
            gr = r_ref[rows, vs]
            on = _head_rms(o, gng_ref[...]) * (gr * jax.nn.sigmoid(gr))
            og_ref[rows, vs] = on.astype(og_ref.dtype)
        return 0

    lax.fori_loop(0, tt // chunk, body, 0)

    @pl.when(cg == pl.num_programs(1) - 1)
    def _():
        for h in range(GLA_HEADS):
            s_ref[h] = st_ref[h].T


def _gla(zz3, gw2, gb, gng, l, s0, tt, chunk):
    B, T, _ = zz3.shape
    nq = GLA_HEADS * GLA_DK
    nv = GLA_HEADS * GLA_DV
    return pl.pallas_call(
        functools.partial(_gla_kernel, chunk=chunk),
        grid=(B, T // tt),
        in_specs=[pl.BlockSpec((None, tt, nq), lambda b, c: (b, c, C_GQ // nq)),
                  pl.BlockSpec((None, tt, nq), lambda b, c: (b, c, C_GK // nq)),
                  pl.BlockSpec((None, tt, nv), lambda b, c: (b, c, C_GV // nv)),
                  pl.BlockSpec((None, tt, nv), lambda b, c: (b, c, C_GR // nv)),
                  pl.BlockSpec((None, tt, HEAD_DIM), lambda b, c: (b, c, C_MISC // HEAD_DIM)),
                  pl.BlockSpec((None, GLA_RANK, nq), lambda b, c: (l, 0, 0)),
                  pl.BlockSpec((None, 1, nq), lambda b, c: (l, 0, 0)),
                  pl.BlockSpec((None, 1, GLA_DV), lambda b, c: (l, 0, 0)),
                  pl.BlockSpec((None, GLA_HEADS, GLA_DK, GLA_DV), lambda b, c: (b, 0, 0, 0))],
        out_specs=[pl.BlockSpec((None, tt, nv), lambda b, c: (b, c, 0)),
                   pl.BlockSpec((None, GLA_HEADS, GLA_DK, GLA_DV), lambda b, c: (b, 0, 0, 0))],
        out_shape=[jax.ShapeDtypeStruct((B, T, nv), BF16),
                   jax.ShapeDtypeStruct((B, GLA_HEADS, GLA_DK, GLA_DV), F32)],
        scratch_shapes=[pltpu.VMEM((tt, nq), F32), pltpu.VMEM((GLA_HEADS, GLA_DV, GLA_DK), F32)],
        compiler_params=_cparams(("parallel", "arbitrary")),
        name="gla",
    )(zz3, zz3, zz3, zz3, zz3, gw2, gb, gng, s0)


def _masked_softmax(s, mask):
    s = jnp.where(mask, s.astype(F32), NEG)
    p = jnp.where(mask, jnp.exp(s - jnp.max(s, axis=-1, keepdims=True)), 0.0)
    return p / jnp.maximum(jnp.sum(p, axis=-1, keepdims=True), 1e-30)


def _group(a, g):
    B, T, H, D = a.shape
    return a.transpose(0, 2, 1, 3).reshape(B, g, H // g, T, D)


def _ungroup(o):
    B, G, R, T, D = o.shape
    return o.reshape(B, G * R, T, D).transpose(0, 2, 1, 3)


def _to_blocks(rows, blk):
    B, L, G, D = rows.shape
    nb = -(-L // blk)
    r = jnp.pad(rows, ((0, 0), (0, nb * blk - L), (0, 0), (0, 0)))
    return r.reshape(B, nb, blk, G, D).transpose(0, 3, 1, 2, 4)


def _rms_j(x, g):
    return x * lax.rsqrt(jnp.mean(x * x, axis=-1, keepdims=True) + EPS) * g


def _compress_j(rows, pe, w1, b1, w2):
    B, L, G, D = rows.shape
    per = CMP_LEN // CMP_STRIDE
    nch = L // CMP_STRIDE
    nc = nch - per + 1
    ch = rows[:, :nch * CMP_STRIDE].reshape(B, nch, CMP_STRIDE, G, D)
    w1r = w1.reshape(per, CMP_STRIDE, D, CMP_HID)
    per_pe = pe.reshape(per, CMP_STRIDE, 1, D)
    hid = b1
    for j in range(per):
        hid = hid + jnp.einsum('bcsgd,sdh->bcgh', ch[:, j:j + nc] + per_pe[j], w1r[j])
    return jnp.einsum('bcgh,hd->bgcd', jax.nn.gelu(hid), w2)


def _decode_attn_jax(zs, pos, past_len, nsa_past, moba_past, win_past, nsa_g, cpos, cw1, cb1, cw2):
    B, T, _ = zs.shape
    hd = lambda c, n: zs[:, :, c:c + n * HEAD_DIM].reshape(B, T, n, HEAD_DIM)
    q_a, q_ar, q_b = hd(C_QA, NSA_HEADS), hd(C_QAR, NSA_HEADS), hd(C_QB, MOBA_HEADS)
    nsa_rows = zs[:, :, C_NSA:C_NSA + 4 * ZT].reshape(B, T, 4, NSA_KV, HEAD_DIM)
    win_rows = zs[:, :, C_WIN:C_WIN + 2 * ZT].reshape(B, T, 2, NSA_KV, HEAD_DIM)
    moba_rows = zs[:, :, C_MOBA:C_MOBA + 2 * ZT].reshape(B, T, 2, MOBA_KV, HEAD_DIM)
    nsa_ctx = jnp.concatenate([nsa_past, nsa_rows], axis=1)
    moba_ctx = jnp.concatenate([moba_past, moba_rows], axis=1)
    kc = _rms_j(_compress_j(nsa_ctx[:, :, 0], cpos[0], cw1[0], cb1[0], cw2[0]), nsa_g[1])
    vc = _compress_j(nsa_ctx[:, :, 1], cpos[1], cw1[1], cb1[1], cw2[1])
    ks = _to_blocks(nsa_ctx[:, :, 2], SLC_BLK)
    vs = _to_blocks(nsa_ctx[:, :, 3], SLC_BLK)
    q_c, q_s = _group(q_a, NSA_KV), _group(q_ar, NSA_KV)
    G = NSA_KV
    C, NS = kc.shape[2], ks.shape[2]
    cmp_end = jnp.arange(C) * CMP_STRIDE + CMP_LEN - 1
    p_c = _masked_softmax(jnp.einsum('bgrqd,bgcd->bgrqc', q_c, kc, precision=HI) * SCALE,
                          cmp_end[None, :] <= pos[:, None])
    o_c = jnp.einsum('bgrqc,bgcd->bgrqd', p_c, vc)
    imp = jnp.einsum('bgrqc,cj->bgqj', p_c, jnp.asarray(_overlap_matrix(C, NS)), precision=HI)
    cur = pos // SLC_BLK
    jb = jnp.arange(NS)[None, :]
    forced = (jb == 0) | (jb == cur[:, None]) | (jb == cur[:, None] - 1)
    score = jnp.where(forced, jnp.inf, jnp.where(jb <= cur[:, None], imp, -jnp.inf))
    n_sel = min(SLC_TOPN, NS)
    _, idx = lax.top_k(score, n_sel)
    bi = jnp.arange(B)[:, None, None, None]
    gi = jnp.arange(G)[None, :, None, None]
    k_sel = ks[bi, gi, idx].reshape(B, G, T, n_sel * SLC_BLK, HEAD_DIM)
    v_sel = vs[bi, gi, idx].reshape(B, G, T, n_sel * SLC_BLK, HEAD_DIM)
    kpos = idx[..., None] * SLC_BLK + jnp.arange(SLC_BLK)
    m_s = (kpos <= pos[:, None, None]).reshape(B, G, T, n_sel * SLC_BLK)
    p_s = _masked_softmax(jnp.einsum('bgrqd,bgqkd->bgrqk', q_s, k_sel) * SCALE, m_s[:, :, None])
    o_s = jnp.einsum('bgrqk,bgqkd->bgrqd', p_s, v_sel)
    win_ctx = jnp.concatenate([win_past, win_rows], axis=1)
    wb = win_past.shape[1]
    kposw = past_len - wb + jnp.arange(wb + T)
    d = pos[:, None] - kposw[None, :]
    kw = win_ctx[:, :, 0].transpose(0, 2, 1, 3)
    vw = win_ctx[:, :, 1].transpose(0, 2, 1, 3)
    p_w = _masked_softmax(jnp.einsum('bgrqd,bgkd->bgrqk', q_s, kw) * SCALE, (d >= 0) & (d < WINDOW))
    o_w = jnp.einsum('bgrqk,bgkd->bgrqd', p_w, vw)
    ngt = jnp.concatenate([zs[:, :, C_MISC:C_MISC + 9], zs[:, :, C_MISC + 128:C_MISC + 137]], axis=-1)
    gate = jax.nn.sigmoid(ngt).reshape(B, T, NSA_HEADS, 3)
    o_a = gate[..., 0:1] * _ungroup(o_c) + gate[..., 1:2] * _ungroup(o_s) + gate[..., 2:3] * _ungroup(o_w)
    kb = _to_blocks(moba_ctx[:, :, 0], MOBA_BLK)
    vb = _to_blocks(moba_ctx[:, :, 1], MOBA_BLK)
    kmean = jnp.mean(kb, axis=3)
    q = _group(q_b, MOBA_KV)
    NB = kb.shape[2]
    own = pos // MOBA_BLK
    s_g = jnp.einsum('bgrqd,bgnd->bgrqn', q, kmean, precision=HI)
    s_g = jnp.where(jnp.arange(NB)[None, :] < own[:, None], s_g, -jnp.inf)
    k_top = min(MOBA_TOPK, NB)
    _, idx = lax.top_k(s_g, k_top)
    ok = idx < own[:, None]
    idx_all = jnp.concatenate([idx, jnp.broadcast_to(own[:, None], idx.shape[:-1] + (1,))], axis=-1)
    ok_all = jnp.concatenate([ok, jnp.ones(idx.shape[:-1] + (1,), bool)], axis=-1)
    bi = jnp.arange(B)[:, None, None, None, None]
    gi = jnp.arange(MOBA_KV)[None, :, None, None, None]
    nk = (k_top + 1) * MOBA_BLK
    R = MOBA_REP
    k_sel = kb[bi, gi, idx_all].reshape(B, MOBA_KV, R, T, nk, HEAD_DIM)
    v_sel = vb[bi, gi, idx_all].reshape(B, MOBA_KV, R, T, nk, HEAD_DIM)
    kpos = idx_all[..., None] * MOBA_BLK + jnp.arange(MOBA_BLK)
    m = (ok_all[..., None] & (kpos <= pos[:, None, None])).reshape(B, MOBA_KV, R, T, nk)
    p = _masked_softmax(jnp.einsum('bgrqd,bgrqkd->bgrqk', q, k_sel) * SCALE, m)
    o_b = jnp.einsum('bgrqk,bgrqkd->bgrqd', p, v_sel)
    return (o_a.reshape(B, T, -1).astype(BF16), _ungroup(o_b).reshape(B, T, -1).astype(BF16))


def _permute_w_in(w_in):
    sizes = (768, 256, 256, 256, 256, 256, 256, 18, 768, 256, 256, 256, 256, 512, 16, 512)
    offs = np.concatenate([[0], np.cumsum(sizes)])
    (nq, nkc, nvc, nks, nvs, nkw, nvw, ngt, mq, mk, mv, gq, gk, gv, ga, gr) = [
        w_in[:, :, int(offs[i]):int(offs[i + 1])] for i in range(len(sizes))]
    L, D = w_in.shape[0], w_in.shape[1]
    z = lambda n: jnp.zeros((L, D, n), w_in.dtype)
    misc = jnp.concatenate([ngt[:, :, :9], z(7), ga, z(96), ngt[:, :, 9:], z(119)], axis=-1)
    wz = jnp.concatenate([nq, nq, nkc, nvc, nks, nvs, nkw, nvw, mq, mk, mv, gq, gv, gr, gk, misc], axis=-1)
    assert wz.shape[-1] == NZ
    return wz.astype(BF16)


def _gain_vector(nsa_g, moba_g):
    L = nsa_g.shape[0]
    gz = jnp.ones((L, NZ), F32)
    put = lambda gz, c, g, n: gz.at[:, c:c + n * HEAD_DIM].set(jnp.tile(g, (1, n)))
    gz = put(gz, C_QA, nsa_g[:, 0], NSA_HEADS)
    gz = put(gz, C_QAR, nsa_g[:, 0], NSA_HEADS)
    gz = put(gz, C_NSA + 2 * ZT, nsa_g[:, 2], NSA_KV)
    gz = put(gz, C_WIN, nsa_g[:, 3], NSA_KV)
    gz = put(gz, C_QB, moba_g[:, 0], MOBA_HEADS)
    gz = put(gz, C_MOBA, moba_g[:, 1], MOBA_KV)
    return gz.reshape(L, 1, NZ)


def _rope_tables(pos):
    half = HEAD_DIM // 2
    inv = ROPE_THETA ** (-jnp.arange(half, dtype=F32) / half)
    ang = pos.astype(F32)[:, None] * inv
    cos, sin = jnp.cos(ang), jnp.sin(ang)
    return jnp.concatenate([cos, cos], axis=-1), jnp.concatenate([-sin, sin], axis=-1)


def kernel(x_prompt, x_sample, cache_nsa, cache_moba, state_nsa_win, state_gla, page_table, c_prompt, c_sample,
           w_ada, b_ada, norm1_g, norm2_g, w_in, nsa_qk_g, nsa_cmp_pos, nsa_cmp_w1, nsa_cmp_b1, nsa_cmp_w2,
           moba_qk_g, gla_w_a2, gla_b_a, gla_norm_g, w_out, w_up, w_down):
    B, T, D = x_prompt.shape
    DB, TD, _ = x_sample.shape
    L = w_in.shape[0]
    past_len = page_table.shape[1] * cache_nsa.shape[2]
    pos_p = jnp.arange(T, dtype=jnp.int32)
    pos_s = past_len + jnp.arange(TD, dtype=jnp.int32)
    MS = DB * TD

    wz = _permute_w_in(w_in)
    gz = _gain_vector(nsa_qk_g, moba_qk_g)
    wo = w_out.astype(BF16)
    wu = w_up.astype(BF16)
    wd = w_down.astype(BF16)
    n1 = norm1_g.reshape(L, 1, D)
    n2 = norm2_g.reshape(L, 1, D)
    b1 = nsa_cmp_b1.reshape(L, 2, 1, CMP_HID)
    nsa_g4 = nsa_qk_g.reshape(L, 4, 1, HEAD_DIM)
    gb = gla_b_a.reshape(L, 1, GLA_HEADS * GLA_DK)
    gng = gla_norm_g.reshape(L, 1, GLA_DV)
    cos_p, sin_p = _rope_tables(pos_p)
    cos_s, sin_s = _rope_tables(jnp.tile(pos_s, DB))

    nc = B + DB
    c_all = jnp.concatenate([c_prompt, c_sample, jnp.zeros((-nc % 8, D), F32)], axis=0)
    mod = _ada(c_all, w_ada, b_ada)

    def mods(l):
        mp = mod[l, :B].reshape(B, 1, 6, D)
        ms = jnp.repeat(mod[l, B:nc], TD, axis=0).reshape(1, MS, 6, D)
        return [mp[:, :, i] for i in range(6)], [ms[:, :, i] for i in range(6)]

    tm_p = 1024
    tpb_p = T // tm_p
    xp = x_prompt.reshape(B * T, D)
    xs = x_sample.reshape(MS, D)
    s0_p = jnp.zeros((B, GLA_HEADS, GLA_DK, GLA_DV), F32)
    outs = [[] for _ in range(8)]
    for l in range(L):
        (sh1, sc1, gt1, sh2, sc2, gt2), (sh1s, sc1s, gt1s, sh2s, sc2s, gt2s) = mods(l)
        zz = _nmm_zz(xp, sc1, sh1, n1, wz, l, gz, cos_p, sin_p, tm_p, tpb_p)
        zz3 = zz.reshape(B, T, NZ)
        kcvc = _compress_prompt(zz3, nsa_cmp_pos, nsa_cmp_w1, b1, nsa_cmp_w2, nsa_g4, l)
        oa = _nsa_prompt(zz3, kcvc)
        ob = _moba_prompt(zz3)
        og, s_fin = _gla(zz3, gla_w_a2, gb, gng, l, s0_p, 512, GLA_CHUNK)
        xp = _outproj(oa.reshape(B * T, -1), ob.reshape(B * T, -1), og.reshape(B * T, -1), wo, l, xp, gt1,
                      tm_p, tpb_p)
        u = _nmm_relu2(xp, sc2, sh2, n2, wu, l, tm_p, tpb_p)
        xp = _down(u, wd, l, xp, gt2, tm_p, tpb_p)
        outs[0].append(zz3[:, :, C_NSA:C_NSA + 4 * ZT].reshape(B, T, 4, NSA_KV, HEAD_DIM))
        outs[2].append(zz3[:, T - min(WINDOW, T):, C_WIN:C_WIN + 2 * ZT].reshape(B, -1, 2, NSA_KV, HEAD_DIM))
        outs[4].append(zz3[:, :, C_MOBA:C_MOBA + 2 * ZT].reshape(B, T, 2, MOBA_KV, HEAD_DIM))
        outs[6].append(s_fin)
        zs = _nmm_zz(xs, sc1s, sh1s, n1, wz, l, gz, cos_s, sin_s, MS, 1)
        zs3 = zs.reshape(DB, TD, NZ)
        gather = lambda pool: pool[page_table].reshape((DB, past_len) + pool.shape[2:])
        oa_s, ob_s = _decode_attn_jax(zs3, pos_s, past_len, gather(cache_nsa[l]), gather(cache_moba[l]),
                                      state_nsa_win[l], nsa_qk_g[l], nsa_cmp_pos[l], nsa_cmp_w1[l],
                                      nsa_cmp_b1[l], nsa_cmp_w2[l])
        og_s, s_fin_s = _gla(zs3, gla_w_a2, gb, gng, l, state_gla[l], TD, TD)
        xs = _outproj(oa_s.reshape(MS, -1), ob_s.reshape(MS, -1), og_s.reshape(MS, -1), wo, l, xs, gt1s, MS, 1)
        us = _nmm_relu2(xs, sc2s, sh2s, n2, wu, l, MS, 1)
        xs = _down(us, wd, l, xs, gt2s, MS, 1)
        win_rows_s = zs3[:, :, C_WIN:C_WIN + 2 * ZT].reshape(DB, TD, 2, NSA_KV, HEAD_DIM)
        win_ctx = jnp.concatenate([state_nsa_win[l], win_rows_s], axis=1)
        outs[1].append(zs3[:, :, C_NSA:C_NSA + 4 * ZT].reshape(DB, TD, 4, NSA_KV, HEAD_DIM))
        outs[3].append(win_ctx[:, -min(WINDOW, past_len + TD):])
        outs[5].append(zs3[:, :, C_MOBA:C_MOBA + 2 * ZT].reshape(DB, TD, 2, MOBA_KV, HEAD_DIM))
        outs[7].append(s_fin_s)
    st = [jnp.stack(o) for o in outs]
    return (xp.reshape(B, T, D), xs.reshape(DB, TD, D), st[0], st[1], st[2], st[3], st[4], st[5], st[6], st[7])
```

```python
import functools

import numpy as np
import jax
import jax.numpy as jnp
from jax import lax
from jax.experimental import pallas as pl
from jax.experimental.pallas import tpu as pltpu

F32 = jnp.float32
BF16 = jnp.bfloat16
HI = lax.Precision.HIGHEST

D_MODEL = 2048
HEAD_DIM = 128
NSA_HEADS = 6
NSA_KV = 2
NSA_REP = NSA_HEADS // NSA_KV
MOBA_HEADS = 6
MOBA_KV = 2
MOBA_REP = MOBA_HEADS // MOBA_KV
GLA_HEADS = 4
GLA_DK = 64
GLA_DV = 128
GLA_RANK = 16
GLA_TAU = 16.0
GLA_CHUNK = 64
CMP_LEN = 32
CMP_STRIDE = 16
CMP_HID = 128
SLC_BLK = 64
SLC_TOPN = 16
WINDOW = 512
MOBA_BLK = 256
MOBA_TOPK = 3
D_FF = 4 * D_MODEL
ROPE_THETA = 10000.0
EPS = 1e-6
SCALE = HEAD_DIM ** -0.5
NEG = -1e30
LANES = 128

ZT = 256
C_QA = 0
C_QAR = 768
C_NSA = 1536
C_WIN = 2560
C_QB = 3072
C_MOBA = 3840
C_GQ = 4352
C_GV = 4608
C_GR = 5120
C_GK = 5632
C_MISC = 5888
NZ = 6144

VMEM_LIMIT = 56 * 1024 * 1024
PAGES_PER_STEP = 16

NT_DIMS = (((1,), (1,)), ((), ()))
TN_DIMS = (((0,), (0,)), ((), ()))


def _cparams(sem):
    return pltpu.CompilerParams(dimension_semantics=sem, vmem_limit_bytes=VMEM_LIMIT)


def _ada_kernel(c_ref, w_ref, b_ref, o_ref):
    c = c_ref[...]
    a = c * jax.nn.sigmoid(c)
    o_ref[...] = jnp.dot(a, w_ref[...], precision=HI, preferred_element_type=F32) + b_ref[...]


def _ada(c_all, w_ada, b_ada):
    L, D, N = w_ada.shape
    R = c_all.shape[0]
    tn = 512
    return pl.pallas_call(
        _ada_kernel,
        grid=(L, N // tn),
        in_specs=[pl.BlockSpec((R, D), lambda l, j: (0, 0)),
                  pl.BlockSpec((None, D, tn), lambda l, j: (l, 0, j)),
                  pl.BlockSpec((None, 1, tn), lambda l, j: (l, 0, j))],
        out_specs=pl.BlockSpec((None, R, tn), lambda l, j: (l, 0, j)),
        out_shape=jax.ShapeDtypeStruct((L, R, N), F32),
        compiler_params=_cparams(("parallel", "parallel")),
        name="ada_mod",
    )(c_all, w_ada, b_ada.reshape(L, 1, N))


def _mod_norm(x_ref, sc_ref, sh_ref, gn_ref, h_ref):
    x = x_ref[...]
    y = x * lax.rsqrt(jnp.mean(x * x, axis=-1, keepdims=True) + EPS) * gn_ref[...]
    h_ref[...] = (y * (1.0 + sc_ref[...]) + sh_ref[...]).astype(BF16)


def _head_rms(a, g):
    return a * lax.rsqrt(jnp.mean(a * a, axis=-1, keepdims=True) + EPS) * g


def _nmm_zz_kernel(x_ref, sc_ref, sh_ref, gn_ref, w_ref, gz_ref, cos_ref, sin_ref, o_ref, h_ref):
    j = pl.program_id(1)

    @pl.when(j == 0)
    def _():
        _mod_norm(x_ref, sc_ref, sh_ref, gn_ref, h_ref)

    acc = jnp.dot(h_ref[...], w_ref[...], preferred_element_type=F32)
    is_norm = j < 3
    is_rope = ((j >= 3) & (j <= 5)) | (j == 8) | (j == 10) | ((j >= 12) & (j <= 15))

    @pl.when(jnp.logical_not(is_norm | is_rope))
    def _():
        o_ref[...] = acc

    @pl.when(is_norm)
    def _():
        for hh in range(ZT // HEAD_DIM):
            sl = slice(hh * HEAD_DIM, (hh + 1) * HEAD_DIM)
            o_ref[:, sl] = _head_rms(acc[:, sl], gz_ref[:, sl])

    @pl.when(is_rope)
    def _():
        for hh in range(ZT // HEAD_DIM):
            sl = slice(hh * HEAD_DIM, (hh + 1) * HEAD_DIM)
            y = _head_rms(acc[:, sl], gz_ref[:, sl])
            o_ref[:, sl] = y * cos_ref[...] + pltpu.roll(y, HEAD_DIM // 2, 1) * sin_ref[...]


def _nmm_relu2_kernel(x_ref, sc_ref, sh_ref, gn_ref, w_ref, o_ref, h_ref):
    @pl.when(pl.program_id(1) == 0)
    def _():
        _mod_norm(x_ref, sc_ref, sh_ref, gn_ref, h_ref)

    acc = jnp.dot(h_ref[...], w_ref[...], preferred_element_type=F32)
    r = jnp.maximum(acc, 0.0)
    o_ref[...] = (r * r).astype(o_ref.dtype)


def _mod_specs(sc, tpb):
    rows = sc.shape[1]
    D = sc.shape[2]
    return pl.BlockSpec((None, rows, D), lambda i, j: (i // tpb, 0, 0))


def _nmm_zz(x, sc, sh, gn, wz, l, gz, cos, sin, tm, tpb):
    M, D = x.shape
    tn = ZT
    return pl.pallas_call(
        _nmm_zz_kernel,
        grid=(M // tm, NZ // tn),
        in_specs=[pl.BlockSpec((tm, D), lambda i, j: (i, 0)),
                  _mod_specs(sc, tpb), _mod_specs(sh, tpb),
                  pl.BlockSpec((None, 1, D), lambda i, j: (l, 0, 0)),
                  pl.BlockSpec((None, D, tn), lambda i, j: (l, 0, j)),
                  pl.BlockSpec((None, 1, tn), lambda i, j: (l, 0, j)),
                  pl.BlockSpec((tm, HEAD_DIM), lambda i, j: (i % tpb, 0)),
                  pl.BlockSpec((tm, HEAD_DIM), lambda i, j: (i % tpb, 0))],
        out_specs=pl.BlockSpec((tm, tn), lambda i, j: (i, j)),
        out_shape=jax.ShapeDtypeStruct((M, NZ), F32),
        scratch_shapes=[pltpu.VMEM((tm, D), BF16)],
        compiler_params=_cparams(("parallel", "arbitrary")),
        name="in_proj",
    )(x, sc, sh, gn, wz, gz, cos, sin)


def _nmm_relu2(x, sc, sh, gn, wu, l, tm, tpb):
    M, D = x.shape
    N = wu.shape[2]
    tn = 512
    return pl.pallas_call(
        _nmm_relu2_kernel,
        grid=(M // tm, N // tn),
        in_specs=[pl.BlockSpec((tm, D), lambda i, j: (i, 0)),
                  _mod_specs(sc, tpb), _mod_specs(sh, tpb),
                  pl.BlockSpec((None, 1, D), lambda i, j: (l, 0, 0)),
                  pl.BlockSpec((None, D, tn), lambda i, j: (l, 0, j))],
        out_specs=pl.BlockSpec((tm, tn), lambda i, j: (i, j)),
        out_shape=jax.ShapeDtypeStruct((M, N), BF16),
        scratch_shapes=[pltpu.VMEM((tm, D), BF16)],
        compiler_params=_cparams(("parallel", "arbitrary")),
        name="mlp_up",
    )(x, sc, sh, gn, wu)


def _outproj_kernel(oa_ref, ob_ref, og_ref, w_ref, x_ref, gt_ref, o_ref):
    na, nb = oa_ref.shape[1], ob_ref.shape[1]
    acc = jnp.dot(oa_ref[...], w_ref[0:na, :], preferred_element_type=F32)
    acc += jnp.dot(ob_ref[...], w_ref[na:na + nb, :], preferred_element_type=F32)
    acc += jnp.dot(og_ref[...], w_ref[na + nb:, :], preferred_element_type=F32)
    o_ref[...] = x_ref[...] + gt_ref[...] * acc


def _outproj(oa, ob, og, wo, l, x, gt, tm, tpb):
    M, D = x.shape
    tn = 512
    rows = gt.shape[1]
    return pl.pallas_call(
        _outproj_kernel,
        grid=(M // tm, D // tn),
        in_specs=[pl.BlockSpec((tm, oa.shape[1]), lambda i, j: (i, 0)),
                  pl.BlockSpec((tm, ob.shape[1]), lambda i, j: (i, 0)),
                  pl.BlockSpec((tm, og.shape[1]), lambda i, j: (i, 0)),
                  pl.BlockSpec((None, D, tn), lambda i, j: (l, 0, j)),
                  pl.BlockSpec((tm, tn), lambda i, j: (i, j)),
                  pl.BlockSpec((None, rows, tn), lambda i, j: (i // tpb, 0, j))],
        out_specs=pl.BlockSpec((tm, tn), lambda i, j: (i, j)),
        out_shape=jax.ShapeDtypeStruct((M, D), F32),
        compiler_params=_cparams(("parallel", "parallel")),
        name="out_proj",
    )(oa, ob, og, wo, x, gt)


def _down_kernel(u_ref, w_ref, x_ref, gt_ref, o_ref, acc_ref):
    k = pl.program_id(2)

    @pl.when(k == 0)
    def _():
        acc_ref[...] = jnp.zeros_like(acc_ref)

    acc_ref[...] += jnp.dot(u_ref[...], w_ref[...], preferred_element_type=F32)

    @pl.when(k == pl.num_programs(2) - 1)
    def _():
        o_ref[...] = x_ref[...] + gt_ref[...] * acc_ref[...]


def _down(u, wd, l, x, gt, tm, tpb):
    M, D = x.shape
    K = u.shape[1]
    tn, tk = 512, 2048
    rows = gt.shape[1]
    return pl.pallas_call(
        _down_kernel,
        grid=(M // tm, D // tn, K // tk),
        in_specs=[pl.BlockSpec((tm, tk), lambda i, j, k: (i, k)),
                  pl.BlockSpec((None, tk, tn), lambda i, j, k: (l, k, j)),
                  pl.BlockSpec((tm, tn), lambda i, j, k: (i, j)),
                  pl.BlockSpec((None, rows, tn), lambda i, j, k: (i // tpb, 0, j))],
        out_specs=pl.BlockSpec((tm, tn), lambda i, j, k: (i, j)),
        out_shape=jax.ShapeDtypeStruct((M, D), F32),
        scratch_shapes=[pltpu.VMEM((tm, tn), F32)],
        compiler_params=_cparams(("parallel", "parallel", "arbitrary")),
        name="mlp_down",
    )(u, wd, x, gt)


def _compress_kernel(rows_ref, pe_ref, w1_ref, b1_ref, w2_ref, g_ref, o_ref, *, nch):
    slot = pl.program_id(1)
    pa = jnp.zeros((nch, CMP_HID), F32)
    pb = jnp.zeros((nch, CMP_HID), F32)
    for s in range(CMP_STRIDE):
        xs = rows_ref[pl.ds(s, nch, stride=CMP_STRIDE), :]
        wa = w1_ref[s * HEAD_DIM:(s + 1) * HEAD_DIM, :]
        wb = w1_ref[(CMP_STRIDE + s) * HEAD_DIM:(CMP_STRIDE + s + 1) * HEAD_DIM, :]
        pa += jnp.dot(xs + pe_ref[s:s + 1, :], wa, precision=HI, preferred_element_type=F32)
        pb += jnp.dot(xs + pe_ref[CMP_STRIDE + s:CMP_STRIDE + s + 1, :], wb, precision=HI,
                      preferred_element_type=F32)
    hid = b1_ref[...] + pa + pltpu.roll(pb, nch - 1, 0)
    tok = jnp.dot(jax.nn.gelu(hid), w2_ref[...], precision=HI, preferred_element_type=F32)

    @pl.when(slot == 0)
    def _():
        o_ref[...] = _head_rms(tok, g_ref[...])

    @pl.when(slot != 0)
    def _():
        o_ref[...] = tok


def _compress_prompt(zz3, pe, w1, b1, w2, gk, l):
    B, T, _ = zz3.shape
    nch = T // CMP_STRIDE
    cb = C_NSA // HEAD_DIM
    return pl.pallas_call(
        functools.partial(_compress_kernel, nch=nch),
        grid=(B, 2, NSA_KV),
        in_specs=[pl.BlockSpec((None, T, HEAD_DIM), lambda b, s, g: (b, 0, cb + 2 * s + g)),
                  pl.BlockSpec((None, None, CMP_LEN, HEAD_DIM), lambda b, s, g: (l, s, 0, 0)),
                  pl.BlockSpec((None, None, CMP_LEN * HEAD_DIM, CMP_HID), lambda b, s, g: (l, s, 0, 0)),
                  pl.BlockSpec((None, None, 1, CMP_HID), lambda b, s, g: (l, s, 0, 0)),
                  pl.BlockSpec((None, None, CMP_HID, HEAD_DIM), lambda b, s, g: (l, s, 0, 0)),
                  pl.BlockSpec((None, None, 1, HEAD_DIM), lambda b, s, g: (l, 1, 0, 0))],
        out_specs=pl.BlockSpec((None, None, None, nch, HEAD_DIM), lambda b, s, g: (s, b, g, 0, 0)),
        out_shape=jax.ShapeDtypeStruct((2, B, NSA_KV, nch, HEAD_DIM), F32),
        compiler_params=_cparams(("parallel", "parallel", "parallel")),
        name="nsa_compress",
    )(zz3, pe, w1, b1, w2, gk)


def _rank_desc(score, n, axis):
    idx = lax.broadcasted_iota(jnp.int32, score.shape, axis)
    rank = jnp.zeros(score.shape, jnp.int32)
    for i in range(n):
        ci = score[:, i:i + 1] if axis == 1 else score[i:i + 1, :]
        beats = (ci > score) | ((ci == score) & (idx > i))
        rank = rank + beats.astype(jnp.int32)
    return rank


def _online_update(carry, s, msk, v):
    m, l, acc = carry
    s = jnp.where(msk, s, NEG)
    m_new = jnp.maximum(m, jnp.max(s, axis=-1, keepdims=True))
    alpha = jnp.exp(m - m_new)
    p = jnp.where(msk, jnp.exp(s - m_new), 0.0)
    l = alpha * l + jnp.sum(p, axis=-1, keepdims=True)
    acc = alpha * acc + jnp.dot(p.astype(BF16), v, preferred_element_type=F32)
    return m_new, l, acc


def _online_init(rows):
    return (jnp.full((rows, 1), NEG, F32), jnp.zeros((rows, 1), F32), jnp.zeros((rows, HEAD_DIM), F32))


def _online_finish(carry):
    _, l, acc = carry
    return acc / jnp.maximum(l, 1e-30)


def _flash_reset(m_ref, l_ref, acc_ref):
    m_ref[...] = jnp.full(m_ref.shape, NEG, F32)
    l_ref[...] = jnp.zeros(l_ref.shape, F32)
    acc_ref[...] = jnp.zeros(acc_ref.shape, F32)


def _flash_step(r, q, k, v, bias, m_ref, l_ref, acc_ref):
    s = lax.dot_general(q, k, NT_DIMS, preferred_element_type=F32) + bias
    m_prev = m_ref[r]
    m_new = jnp.maximum(m_prev, jnp.max(s, axis=-1, keepdims=True))
    alpha = jnp.exp(m_prev - m_new)
    p = jnp.exp(s - jnp.concatenate([m_new] * (s.shape[1] // LANES), axis=1))
    l_ref[r] = alpha * l_ref[r] + jnp.sum(p, axis=-1, keepdims=True)
    acc_ref[r] = alpha * acc_ref[r] + jnp.dot(p.astype(BF16), v, preferred_element_type=F32)
    m_ref[r] = m_new


def _flash_result(r, l_ref, acc_ref):
    return acc_ref[r] / jnp.maximum(l_ref[r], 1e-30)


def _stack_heads(q_ref, g, rep):
    return jnp.concatenate([q_ref[:, (g * rep + r) * HEAD_DIM:(g * rep + r + 1) * HEAD_DIM] for r in range(rep)],
                           axis=0)


def _nsa_prompt_kernel(qa_ref, qar_ref, kc_ref, vc_ref, ks_ref, vs_ref, kw_ref, vw_ref, gt_ref, ovt_ref, o_ref,
                       m_ref, l_ref, acc_ref, *, tq, n_sel):
    qi = pl.program_id(2)
    R = NSA_REP
    C = kc_ref.shape[0]
    NS = ovt_ref.shape[0]
    tk = tq
    q0 = qi * tq
    qpos = q0 + lax.broadcasted_iota(jnp.int32, (tq, 1), 0)
    qpos_l = q0 + lax.broadcasted_iota(jnp.int32, (1, tq), 1)

    kc = kc_ref[...]
    vc = vc_ref[...].astype(BF16)
    cmask = (lax.broadcasted_iota(jnp.int32, (C, 1), 0) * CMP_STRIDE + (CMP_LEN - 1)) <= qpos_l
    imp = jnp.zeros((NS, tq), F32)
    o_c = []
    for r in range(R):
        q = qa_ref[:, r * HEAD_DIM:(r + 1) * HEAD_DIM]
        s = lax.dot_general(kc, q, NT_DIMS, precision=HI, preferred_element_type=F32) * SCALE
        s = jnp.where(cmask, s, NEG)
        p = jnp.where(cmask, jnp.exp(s - jnp.max(s, axis=0, keepdims=True)), 0.0)
        p = p / jnp.maximum(jnp.sum(p, axis=0, keepdims=True), 1e-30)
        o_c.append(lax.dot_general(p.astype(BF16), vc, TN_DIMS, preferred_element_type=F32))
        imp = imp + jnp.dot(ovt_ref[...], p, precision=HI, preferred_element_type=F32)

    jb = lax.broadcasted_iota(jnp.int32, (NS, 1), 0)
    cur = qpos_l // SLC_BLK
    forced = (jb == 0) | (jb == cur) | (jb == cur - 1)
    score = jnp.where(forced, jnp.inf, jnp.where(jb <= cur, imp, -jnp.inf))
    sel = (_rank_desc(score, NS, 0) < n_sel).astype(F32).T.astype(BF16)

    qs = [(qar_ref[:, r * HEAD_DIM:(r + 1) * HEAD_DIM] * SCALE).astype(BF16) for r in range(R)]

    _flash_reset(m_ref, l_ref, acc_ref)

    def sel_body(kt, _):
        k0 = pl.multiple_of(kt * tk, tk)
        k = ks_ref[pl.ds(k0, tk), :].astype(BF16)
        v = vs_ref[pl.ds(k0, tk), :].astype(BF16)
        kpos_e = k0 + lax.broadcasted_iota(jnp.int32, (NS, tk), 1)
        expand = (lax.broadcasted_iota(jnp.int32, (NS, tk), 0) == kpos_e // SLC_BLK).astype(BF16)
        selx = jnp.dot(sel, expand, preferred_element_type=F32)
        kpos = k0 + lax.broadcasted_iota(jnp.int32, (1, tk), 1)
        bias = jnp.where((selx > 0.5) & (kpos <= qpos), 0.0, NEG)
        for r in range(R):
            _flash_step(r, qs[r], k, v, bias, m_ref, l_ref, acc_ref)
        return 0

    lax.fori_loop(0, qi + 1, sel_body, 0)
    o_s = [_flash_result(r, l_ref, acc_ref) for r in range(R)]

    _flash_reset(m_ref, l_ref, acc_ref)
    for w in range(WINDOW // tk + 1):
        kt = qi - (WINDOW // tk) + w
        k0 = pl.multiple_of(jnp.maximum(kt, 0) * tk, tk)
        k = kw_ref[pl.ds(k0, tk), :].astype(BF16)
        v = vw_ref[pl.ds(k0, tk), :].astype(BF16)
        kbase = jnp.where(kt >= 0, k0, -(1 << 30))
        d = qpos - (kbase + lax.broadcasted_iota(jnp.int32, (1, tk), 1))
        bias = jnp.where((d >= 0) & (d < WINDOW), 0.0, NEG)
        for r in range(R):
            _flash_step(r, qs[r], k, v, bias, m_ref, l_ref, acc_ref)

    gate = jax.nn.sigmoid(gt_ref[...])
    for r in range(R):
        o = (gate[:, 3 * r:3 * r + 1] * o_c[r] + gate[:, 3 * r + 1:3 * r + 2] * o_s[r]
             + gate[:, 3 * r + 2:3 * r + 3] * _flash_result(r, l_ref, acc_ref))
        o_ref[:, r * HEAD_DIM:(r + 1) * HEAD_DIM] = o.astype(o_ref.dtype)


def _overlap_matrix(C, NS):
    ci, sj = np.arange(C)[:, None], np.arange(NS)[None, :]
    return ((ci * CMP_STRIDE < (sj + 1) * SLC_BLK) & (ci * CMP_STRIDE + CMP_LEN > sj * SLC_BLK)).astype(np.float32)


def _nsa_prompt(zz3, kcvc):
    B, T, _ = zz3.shape
    tq = 256
    C = kcvc.shape[3]
    NS = -(-T // SLC_BLK)
    ovt = jnp.asarray(_overlap_matrix(C, NS).T)
    gw = NSA_REP * HEAD_DIM
    col = lambda c: c // HEAD_DIM
    kv_spec = lambda c: pl.BlockSpec((None, T, HEAD_DIM), lambda b, g, i: (b, 0, col(c) + g))
    return pl.pallas_call(
        functools.partial(_nsa_prompt_kernel, tq=tq, n_sel=min(SLC_TOPN, NS)),
        grid=(B, NSA_KV, T // tq),
        in_specs=[pl.BlockSpec((None, tq, gw), lambda b, g, i: (b, i, C_QA // gw + g)),
                  pl.BlockSpec((None, tq, gw), lambda b, g, i: (b, i, C_QAR // gw + g)),
                  pl.BlockSpec((None, None, None, C, HEAD_DIM), lambda b, g, i: (0, b, g, 0, 0)),
                  pl.BlockSpec((None, None, None, C, HEAD_DIM), lambda b, g, i: (1, b, g, 0, 0)),
                  kv_spec(C_NSA + 2 * ZT), kv_spec(C_NSA + 3 * ZT), kv_spec(C_WIN), kv_spec(C_WIN + ZT),
                  pl.BlockSpec((None, tq, HEAD_DIM), lambda b, g, i: (b, i, col(C_MISC) + g)),
                  pl.BlockSpec((NS, C), lambda b, g, i: (0, 0))],
        out_specs=pl.BlockSpec((None, tq, gw), lambda b, g, i: (b, i, g)),
        out_shape=jax.ShapeDtypeStruct((B, T, NSA_HEADS * HEAD_DIM), BF16),
        scratch_shapes=[pltpu.VMEM((NSA_REP, tq, LANES), F32), pltpu.VMEM((NSA_REP, tq, LANES), F32),
                        pltpu.VMEM((NSA_REP, tq, HEAD_DIM), F32)],
        compiler_params=_cparams(("parallel", "parallel", "arbitrary")),
        name="nsa_prompt",
    )(zz3, zz3, kcvc, kcvc, zz3, zz3, zz3, zz3, zz3, ovt)


def _moba_prompt_kernel(q_ref, kb_ref, vb_ref, o_ref, kmean_ref, m_ref, l_ref, acc_ref, *, tq, k_top):
    qi = pl.program_id(2)
    R = MOBA_REP
    NB = kmean_ref.shape[0]
    tk = MOBA_BLK
    own = qi

    @pl.when(qi == 0)
    def _():
        kmean_ref[...] = jnp.sum(kb_ref[...].reshape(NB, MOBA_BLK, HEAD_DIM), axis=1) * (1.0 / MOBA_BLK)

    nbi = lax.broadcasted_iota(jnp.int32, (1, NB), 1)
    past = nbi < own
    hidden = []
    for r in range(R):
        q = q_ref[:, r * HEAD_DIM:(r + 1) * HEAD_DIM]
        sg = lax.dot_general(q, kmean_ref[...], NT_DIMS, precision=HI, preferred_element_type=F32)
        sg = jnp.where(past, sg, -jnp.inf)
        hidden.append(jnp.where((_rank_desc(sg, NB, 1) < k_top) & past, 0.0, NEG))
    qs = [(q_ref[:, r * HEAD_DIM:(r + 1) * HEAD_DIM] * SCALE).astype(BF16) for r in range(R)]

    _flash_reset(m_ref, l_ref, acc_ref)

    def body(n, _):
        k0 = pl.multiple_of(n * tk, tk)
        k = kb_ref[pl.ds(k0, tk), :].astype(BF16)
        v = vb_ref[pl.ds(k0, tk), :].astype(BF16)
        for r in range(R):
            bias = jnp.sum(jnp.where(nbi == n, hidden[r], 0.0), axis=-1, keepdims=True)
            _flash_step(r, qs[r], k, v, bias, m_ref, l_ref, acc_ref)
        return 0

    lax.fori_loop(0, qi, body, 0)
    k0 = pl.multiple_of(qi * tk, tk)
    k = kb_ref[pl.ds(k0, tk), :].astype(BF16)
    v = vb_ref[pl.ds(k0, tk), :].astype(BF16)
    ri = lax.broadcasted_iota(jnp.int32, (tq, tk), 0)
    ci = lax.broadcasted_iota(jnp.int32, (tq, tk), 1)
    bias = jnp.where(ci <= ri, 0.0, NEG)
    for r in range(R):
        _flash_step(r, qs[r], k, v, bias, m_ref, l_ref, acc_ref)
        o_ref[:, r * HEAD_DIM:(r + 1) * HEAD_DIM] = _flash_result(r, l_ref, acc_ref).astype(o_ref.dtype)


def _moba_prompt(zz3):
    B, T, _ = zz3.shape
    tq = MOBA_BLK
    NB = T // MOBA_BLK
    gw = MOBA_REP * HEAD_DIM
    col = lambda c: c // HEAD_DIM
    return pl.pallas_call(
        functools.partial(_moba_prompt_kernel, tq=tq, k_top=min(MOBA_TOPK, NB)),
        grid=(B, MOBA_KV, T // tq),
        in_specs=[pl.BlockSpec((None, tq, gw), lambda b, g, i: (b, i, C_QB // gw + g)),
                  pl.BlockSpec((None, T, HEAD_DIM), lambda b, g, i: (b, 0, col(C_MOBA) + g)),
                  pl.BlockSpec((None, T, HEAD_DIM), lambda b, g, i: (b, 0, col(C_MOBA + ZT) + g))],
        out_specs=pl.BlockSpec((None, tq, gw), lambda b, g, i: (b, i, g)),
        out_shape=jax.ShapeDtypeStruct((B, T, MOBA_HEADS * HEAD_DIM), BF16),
        scratch_shapes=[pltpu.VMEM((NB, HEAD_DIM), F32),
                        pltpu.VMEM((MOBA_REP, tq, LANES), F32), pltpu.VMEM((MOBA_REP, tq, LANES), F32),
                        pltpu.VMEM((MOBA_REP, tq, HEAD_DIM), F32)],
        compiler_params=_cparams(("parallel", "parallel", "arbitrary")),
        name="moba_prompt",
    )(zz3, zz3, zz3)


def _gla_kernel(q_ref, k_ref, v_ref, r_ref, misc_ref, gw2_ref, gb_ref, gng_ref, s0_ref, og_ref, s_ref,
                lg_ref, st_ref, *, chunk):
    cg = pl.program_id(1)
    tt = q_ref.shape[0]

    @pl.when(cg == 0)
    def _():
        for h in range(GLA_HEADS):
            st_ref[h] = s0_ref[h].T

    ga = misc_ref[:, 16:16 + GLA_RANK]
    pre = jnp.dot(ga, gw2_ref[...], precision=HI, preferred_element_type=F32) + gb_ref[...]
    lg_ref[...] = jax.nn.log_sigmoid(pre) * (1.0 / GLA_TAU)

    ri = lax.broadcasted_iota(jnp.int32, (chunk, chunk), 0)
    ci = lax.broadcasted_iota(jnp.int32, (chunk, chunk), 1)
    causal = ci <= ri
    tri = causal.astype(F32)
    mid = chunk // 2

    def body(c, _):
        r0 = pl.multiple_of(c * chunk, chunk)
        rows = pl.ds(r0, chunk)
        b_all = jnp.dot(tri, lg_ref[rows, :], precision=HI, preferred_element_type=F32)
        q_all = q_ref[rows, :] * (GLA_DK ** -0.5)
        k_all = k_ref[rows, :]
        v_all = v_ref[rows, :]
        r_all = r_ref[rows, :]
        st_all = [st_ref[h] for h in range(GLA_HEADS)]
        st_new, o_new = [], []
        for h in range(GLA_HEADS):
            ks = slice(h * GLA_DK, (h + 1) * GLA_DK)
            vs = slice(h * GLA_DV, (h + 1) * GLA_DV)
            b = b_all[:, ks]
            bm = b[mid:mid + 1, :]
            bl = b[chunk - 1:chunk, :]
            q = q_all[:, ks]
            k = k_all[:, ks]
            vb = v_all[:, vs].astype(BF16)
            a = lax.dot_general((q * jnp.exp(b - bm)).astype(BF16), (k * jnp.exp(bm - b)).astype(BF16), NT_DIMS,
                                preferred_element_type=F32)
            a = jnp.where(causal, a, 0.0)
            st = st_all[h]
            o = lax.dot_general((q * jnp.exp(b)).astype(BF16), st.astype(BF16), NT_DIMS,
                                preferred_element_type=F32)
            o = o + jnp.dot(a.astype(BF16), vb, preferred_element_type=F32)
            kd = (k * jnp.exp(bl - b)).astype(BF16)
            st_new.append(st * jnp.exp(bl) + lax.dot_general(vb, kd, TN_DIMS, preferred_element_type=F32))
            gr = r_all[:, vs]
            o_new.append((_head_rms(o, gng_ref[...]) * (gr * jax.nn.sigmoid(gr))).astype(og_ref.dtype))
        for h in range(GLA_HEADS):
            st_ref[h] = st_new[h]
        og_ref[rows, :] = jnp.concatenate(o_new, axis=1)
        return 0

    lax.fori_loop(0, tt // chunk, body, 0)

    @pl.when(cg == pl.num_programs(1) - 1)
    def _():
        for h in range(GLA_HEADS):
            s_ref[h] = st_ref[h].T


def _gla(zz3, gw2, gb, gng, l, s0, tt, chunk):
    B, T, _ = zz3.shape
    nq = GLA_HEADS * GLA_DK
    nv = GLA_HEADS * GLA_DV
    return pl.pallas_call(
        functools.partial(_gla_kernel, chunk=chunk),
        grid=(B, T // tt),
        in_specs=[pl.BlockSpec((None, tt, nq), lambda b, c: (b, c, C_GQ // nq)),
                  pl.BlockSpec((None, tt, nq), lambda b, c: (b, c, C_GK // nq)),
                  pl.BlockSpec((None, tt, nv), lambda b, c: (b, c, C_GV // nv)),
                  pl.BlockSpec((None, tt, nv), lambda b, c: (b, c, C_GR // nv)),
                  pl.BlockSpec((None, tt, HEAD_DIM), lambda b, c: (b, c, C_MISC // HEAD_DIM)),
                  pl.BlockSpec((None, GLA_RANK, nq), lambda b, c: (l, 0, 0)),
                  pl.BlockSpec((None, 1, nq), lambda b, c: (l, 0, 0)),
                  pl.BlockSpec((None, 1, GLA_DV), lambda b, c: (l, 0, 0)),
                  pl.BlockSpec((None, GLA_HEADS, GLA_DK, GLA_DV), lambda b, c: (b, 0, 0, 0))],
        out_specs=[pl.BlockSpec((None, tt, nv), lambda b, c: (b, c, 0)),
                   pl.BlockSpec((None, GLA_HEADS, GLA_DK, GLA_DV), lambda b, c: (b, 0, 0, 0))],
        out_shape=[jax.ShapeDtypeStruct((B, T, nv), BF16),
                   jax.ShapeDtypeStruct((B, GLA_HEADS, GLA_DK, GLA_DV), F32)],
        scratch_shapes=[pltpu.VMEM((tt, nq), F32), pltpu.VMEM((GLA_HEADS, GLA_DV, GLA_DK), F32)],
        compiler_params=_cparams(("parallel", "arbitrary")),
        name="gla",
    )(zz3, zz3, zz3, zz3, zz3, gw2, gb, gng, s0)


def _page_copies(cache_ref, pt_ref, l, buf_ref, sem_ref, c0, b, pg, slot):
    page = cache_ref.shape[2]
    ncb, cw = buf_ref.shape[1], buf_ref.shape[3]
    return [pltpu.make_async_copy(cache_ref.at[l, pt_ref[b, pg * PAGES_PER_STEP + k], :, pl.ds(c0 + cb * cw, cw)],
                                  buf_ref.at[slot, cb, pl.ds(k * page, page), :],
                                  sem_ref.at[slot]) for k in range(PAGES_PER_STEP) for cb in range(ncb)]


def _paged_fetch(cache_ref, pt_ref, l, buf_ref, sem_ref, c0):
    b, pg = pl.program_id(0), pl.program_id(1)
    npg = pl.num_programs(1)
    i = b * npg + pg
    slot = i % 2
    copies = functools.partial(_page_copies, cache_ref, pt_ref, l, buf_ref, sem_ref, c0)

    @pl.when(i == 0)
    def _():
        for c in copies(b, pg, slot):
            c.start()

    @pl.when(i + 1 < pl.num_programs(0) * npg)
    def _():
        n = i + 1
        for c in copies(n // npg, n % npg, 1 - slot):
            c.start()

    for c in copies(b, pg, slot):
        c.wait()
    return slot


def _decode_qpos(past_len, td, rep):
    t = lax.broadcasted_iota(jnp.int32, (rep * td, 1), 0) % td
    return past_len + t


def _nsa_dec_select_kernel(pt_ref, qa_ref, cache_ref, wp_ref, pe_ref, w1_ref, b1_ref, w2_ref, gk_ref, ov_ref,
                           oc_ref, sel_ref, buf_ref, sem_ref, pab_ref, *, l, past_len, n_blocks, n_sel):
    pg = pl.program_id(1)
    slot = _paged_fetch(cache_ref, pt_ref, l, buf_ref, sem_ref, 0)
    n = buf_ref.shape[2] // CMP_STRIDE
    C = pab_ref.shape[2]
    td = qa_ref.shape[0]
    R = NSA_REP
    NSP = ov_ref.shape[1]

    for kv in range(2):
        acc = jnp.zeros((NSA_KV * n, 2 * CMP_HID), F32)
        for sp in range(CMP_STRIDE // 2):
            parts = []
            for g in range(NSA_KV):
                cb = kv * NSA_KV + g
                xa = buf_ref[slot, cb, pl.ds(2 * sp, n, stride=CMP_STRIDE), :]
                xb = buf_ref[slot, cb, pl.ds(2 * sp + 1, n, stride=CMP_STRIDE), :]
                parts.append(jnp.concatenate([xa, xb], axis=1))
            x = jnp.concatenate(parts, axis=0).astype(BF16)
            acc += jnp.dot(x, wp_ref[kv, sp], preferred_element_type=F32)
        for g in range(NSA_KV):
            pab_ref[kv, g, pl.ds(pl.multiple_of(pg * n, n), n), :] = acc[g * n:(g + 1) * n]

    @pl.when(pg == pl.num_programs(1) - 1)
    def _():
        qpos = _decode_qpos(past_len, td, 1)
        qpos3 = _decode_qpos(past_len, td, R)
        cmask = (lax.broadcasted_iota(jnp.int32, (1, C), 1) * CMP_STRIDE + (CMP_LEN - 1)) <= qpos3
        toks = []
        for kv in range(2):
            pe8 = jnp.broadcast_to(pe_ref[kv], (8, CMP_LEN * HEAD_DIM))
            const = b1_ref[kv] + jnp.dot(pe8, w1_ref[kv], precision=HI, preferred_element_type=F32)[0:1]
            per_g = []
            for g in range(NSA_KV):
                pab = pab_ref[kv, g]
                hid = const + pab[:, :CMP_HID] + pltpu.roll(pab[:, CMP_HID:], C - 1, 0)
                tok = jnp.dot(jax.nn.gelu(hid), w2_ref[kv], precision=HI, preferred_element_type=F32)
                per_g.append(_head_rms(tok, gk_ref[...]) if kv == 0 else tok)
            toks.append(per_g)
        jb = lax.broadcasted_iota(jnp.int32, (1, NSP), 1)
        cur = qpos // SLC_BLK
        forced = (jb == 0) | (jb == cur) | (jb == cur - 1)
        for g in range(NSA_KV):
            kc, vc = toks[0][g], toks[1][g]
            q3 = _stack_heads(qa_ref, g, R)
            s = lax.dot_general(q3, kc, NT_DIMS, precision=HI, preferred_element_type=F32) * SCALE
            s = jnp.where(cmask, s, NEG)
            p = jnp.where(cmask, jnp.exp(s - jnp.max(s, axis=-1, keepdims=True)), 0.0)
            p = p / jnp.maximum(jnp.sum(p, axis=-1, keepdims=True), 1e-30)
            oc = jnp.dot(p.astype(BF16), vc.astype(BF16), preferred_element_type=F32)
            imp = jnp.zeros((td, NSP), F32)
            for r in range(R):
                h = g * R + r
                oc_ref[:, h * HEAD_DIM:(h + 1) * HEAD_DIM] = oc[r * td:(r + 1) * td]
                imp = imp + jnp.dot(p[r * td:(r + 1) * td], ov_ref[...], precision=HI, preferred_element_type=F32)
            score = jnp.where(forced, jnp.inf, jnp.where(jb <= cur, imp, -jnp.inf))
            sel = (_rank_desc(score, n_blocks, 1) < n_sel).astype(F32)
            sel_ref[g] = jnp.concatenate([sel] * R, axis=0)


def _nsa_dec_select(page_table, zs3, cache4, wp, pe, w1, b1, w2, gk, l, past_len):
    DB, TD, _ = zs3.shape
    n_pages = page_table.shape[1]
    page = cache4.shape[2]
    npg = n_pages // PAGES_PER_STEP
    C = past_len // CMP_STRIDE
    NS = -(-(past_len + TD) // SLC_BLK)
    NSP = -(-NS // LANES) * LANES
    ov = jnp.asarray(_overlap_matrix(C, NSP))
    qw = NSA_HEADS * HEAD_DIM
    rows = PAGES_PER_STEP * page
    grid_spec = pltpu.PrefetchScalarGridSpec(
        num_scalar_prefetch=1,
        grid=(DB, npg),
        in_specs=[pl.BlockSpec((None, TD, qw), lambda b, p, pt: (b, 0, C_QA // qw)),
                  pl.BlockSpec(memory_space=pl.ANY),
                  pl.BlockSpec((None, 2, CMP_STRIDE // 2, 2 * HEAD_DIM, 2 * CMP_HID), lambda b, p, pt: (l, 0, 0, 0, 0)),
                  pl.BlockSpec((None, 2, 1, CMP_LEN * HEAD_DIM), lambda b, p, pt: (l, 0, 0, 0)),
                  pl.BlockSpec((None, 2, CMP_LEN * HEAD_DIM, CMP_HID), lambda b, p, pt: (l, 0, 0, 0)),
                  pl.BlockSpec((None, 2, 1, CMP_HID), lambda b, p, pt: (l, 0, 0, 0)),
                  pl.BlockSpec((None, 2, CMP_HID, HEAD_DIM), lambda b, p, pt: (l, 0, 0, 0)),
                  pl.BlockSpec((None, None, 1, HEAD_DIM), lambda b, p, pt: (l, 1, 0, 0)),
                  pl.BlockSpec((C, NSP), lambda b, p, pt: (0, 0))],
        out_specs=[pl.BlockSpec((None, TD, qw), lambda b, p, pt: (b, 0, 0)),
                   pl.BlockSpec((None, NSA_KV, NSA_REP * TD, NSP), lambda b, p, pt: (b, 0, 0, 0))],
        scratch_shapes=[pltpu.VMEM((2, 2 * NSA_KV, rows, HEAD_DIM), F32), pltpu.SemaphoreType.DMA((2,)),
                        pltpu.VMEM((2, NSA_KV, C, 2 * CMP_HID), F32)])
    return pl.pallas_call(
        functools.partial(_nsa_dec_select_kernel, l=l, past_len=past_len, n_blocks=NS, n_sel=min(SLC_TOPN, NS)),
        grid_spec=grid_spec,
        out_shape=[jax.ShapeDtypeStruct((DB, TD, qw), F32),
                   jax.ShapeDtypeStruct((DB, NSA_KV, NSA_REP * TD, NSP), F32)],
        compiler_params=_cparams(("arbitrary", "arbitrary")),
        name="nsa_dec_select",
    )(page_table, zs3, cache4, wp, pe, w1, b1, w2, gk, ov)


def _paged_attn_kernel(pt_ref, q_ref, sel_ref, kn_ref, vn_ref, cache_ref, o_ref, buf_ref, sem_ref,
                       m_ref, l_ref, acc_ref, *, l, c0, blk, rep):
    pg = pl.program_id(1)
    slot = _paged_fetch(cache_ref, pt_ref, l, buf_ref, sem_ref, c0)
    tk = buf_ref.shape[2]
    td = q_ref.shape[0]
    NBP = sel_ref.shape[2]
    G = sel_ref.shape[0]

    @pl.when(pg == 0)
    def _():
        _flash_reset(m_ref, l_ref, acc_ref)

    kpos = pg * tk + lax.broadcasted_iota(jnp.int32, (NBP, tk), 1)
    expand = (lax.broadcasted_iota(jnp.int32, (NBP, tk), 0) == kpos // blk).astype(BF16)
    qs = [(_stack_heads(q_ref, g, rep) * SCALE).astype(BF16) for g in range(G)]
    for g in range(G):
        k = buf_ref[slot, 0, :, g * HEAD_DIM:(g + 1) * HEAD_DIM].astype(BF16)
        v = buf_ref[slot, 0, :, (G + g) * HEAD_DIM:(G + g + 1) * HEAD_DIM].astype(BF16)
        s = lax.dot_general(qs[g], k, NT_DIMS, preferred_element_type=F32)
        selx = jnp.dot(sel_ref[g].astype(BF16), expand, preferred_element_type=F32)
        m_ref[g], l_ref[g], acc_ref[g] = _online_update((m_ref[g], l_ref[g], acc_ref[g]), s, selx > 0.5, v)

    @pl.when(pg == pl.num_programs(1) - 1)
    def _():
        t = lax.broadcasted_iota(jnp.int32, (rep * td, 1), 0) % td
        causal = lax.broadcasted_iota(jnp.int32, (1, td), 1) <= t
        for g in range(G):
            k = kn_ref[:, g * HEAD_DIM:(g + 1) * HEAD_DIM].astype(BF16)
            v = vn_ref[:, g * HEAD_DIM:(g + 1) * HEAD_DIM].astype(BF16)
            s = lax.dot_general(qs[g], k, NT_DIMS, preferred_element_type=F32)
            o = _online_finish(_online_update((m_ref[g], l_ref[g], acc_ref[g]), s, causal, v))
            for r in range(rep):
                h = g * rep + r
                o_ref[:, h * HEAD_DIM:(h + 1) * HEAD_DIM] = o[r * td:(r + 1) * td]


def _paged_attn(page_table, zs3, sel, cache4, l, c_q, c_k, c_v, c0, blk, name):
    DB, TD, _ = zs3.shape
    G, rows_q, NBP = sel.shape[1:]
    rep = rows_q // TD
    n_pages = page_table.shape[1]
    page = cache4.shape[2]
    npg = n_pages // PAGES_PER_STEP
    qw = G * rep * HEAD_DIM
    kw = G * HEAD_DIM
    rows = PAGES_PER_STEP * page
    grid_spec = pltpu.PrefetchScalarGridSpec(
        num_scalar_prefetch=1,
        grid=(DB, npg),
        in_specs=[pl.BlockSpec((None, TD, qw), lambda b, p, pt: (b, 0, c_q // qw)),
                  pl.BlockSpec((None, G, rows_q, NBP), lambda b, p, pt: (b, 0, 0, 0)),
                  pl.BlockSpec((None, TD, kw), lambda b, p, pt: (b, 0, c_k // kw)),
                  pl.BlockSpec((None, TD, kw), lambda b, p, pt: (b, 0, c_v // kw)),
                  pl.BlockSpec(memory_space=pl.ANY)],
        out_specs=pl.BlockSpec((None, TD, qw), lambda b, p, pt: (b, 0, 0)),
        scratch_shapes=[pltpu.VMEM((2, 1, rows, 2 * kw), F32), pltpu.SemaphoreType.DMA((2,)),
                        pltpu.VMEM((G, rows_q, 1), F32), pltpu.VMEM((G, rows_q, 1), F32),
                        pltpu.VMEM((G, rows_q, HEAD_DIM), F32)])
    return pl.pallas_call(
        functools.partial(_paged_attn_kernel, l=l, c0=c0, blk=blk, rep=rep),
        grid_spec=grid_spec,
        out_shape=jax.ShapeDtypeStruct((DB, TD, qw), F32),
        compiler_params=_cparams(("arbitrary", "arbitrary")),
        name=name,
    )(page_table, zs3, sel, zs3, zs3, cache4)


def _nsa_dec_combine_kernel(q_ref, oc_ref, os_ref, win_ref, wn_ref, gt_ref, o_ref):
    td = q_ref.shape[0]
    wb = win_ref.shape[0]
    R = NSA_REP
    t = lax.broadcasted_iota(jnp.int32, (R * td, 1), 0) % td
    d_buf = wb + t - lax.broadcasted_iota(jnp.int32, (1, wb), 1)
    d_new = t - lax.broadcasted_iota(jnp.int32, (1, td), 1)
    for g in range(NSA_KV):
        ksl = slice(g * HEAD_DIM, (g + 1) * HEAD_DIM)
        vsl = slice((NSA_KV + g) * HEAD_DIM, (NSA_KV + g + 1) * HEAD_DIM)
        q3 = (_stack_heads(q_ref, g, R) * SCALE).astype(BF16)
        carry = _online_init(R * td)
        s = lax.dot_general(q3, win_ref[:, ksl].astype(BF16), NT_DIMS, preferred_element_type=F32)
        carry = _online_update(carry, s, (d_buf >= 0) & (d_buf < WINDOW), win_ref[:, vsl].astype(BF16))
        s = lax.dot_general(q3, wn_ref[:, ksl].astype(BF16), NT_DIMS, preferred_element_type=F32)
        carry = _online_update(carry, s, (d_new >= 0) & (d_new < WINDOW), wn_ref[:, vsl].astype(BF16))
        o_w = _online_finish(carry)
        gate = jax.nn.sigmoid(gt_ref[:, g * LANES:(g + 1) * LANES])
        for r in range(R):
            cols = slice((g * R + r) * HEAD_DIM, (g * R + r + 1) * HEAD_DIM)
            o = (gate[:, 3 * r:3 * r + 1] * oc_ref[:, cols] + gate[:, 3 * r + 1:3 * r + 2] * os_ref[:, cols]
                 + gate[:, 3 * r + 2:3 * r + 3] * o_w[r * td:(r + 1) * td])
            o_ref[:, cols] = o.astype(o_ref.dtype)


def _nsa_dec_combine(zs3, oc, o_s, win4, l):
    DB, TD, _ = zs3.shape
    qw = NSA_HEADS * HEAD_DIM
    wb, ww = win4.shape[2], win4.shape[3]
    return pl.pallas_call(
        _nsa_dec_combine_kernel,
        grid=(DB,),
        in_specs=[pl.BlockSpec((None, TD, qw), lambda b: (b, 0, C_QAR // qw)),
                  pl.BlockSpec((None, TD, qw), lambda b: (b, 0, 0)),
                  pl.BlockSpec((None, TD, qw), lambda b: (b, 0, 0)),
                  pl.BlockSpec((None, None, wb, ww), lambda b: (l, b, 0, 0)),
                  pl.BlockSpec((None, TD, ww), lambda b: (b, 0, C_WIN // ww)),
                  pl.BlockSpec((None, TD, ZT), lambda b: (b, 0, C_MISC // ZT))],
        out_specs=pl.BlockSpec((None, TD, qw), lambda b: (b, 0, 0)),
        out_shape=jax.ShapeDtypeStruct((DB, TD, qw), BF16),
        compiler_params=_cparams(("parallel",)),
        name="nsa_dec_combine",
    )(zs3, oc, o_s, win4, zs3, zs3)


def _moba_dec_select_kernel(pt_ref, q_ref, cache_ref, sel_ref, buf_ref, sem_ref, kmean_ref,
                            *, l, past_len, k_top):
    pg = pl.program_id(1)
    slot = _paged_fetch(cache_ref, pt_ref, l, buf_ref, sem_ref, 0)
    bps = buf_ref.shape[2] // MOBA_BLK
    td = q_ref.shape[0]
    NBP = kmean_ref.shape[1]
    R = MOBA_REP

    @pl.when(pg == 0)
    def _():
        kmean_ref[...] = jnp.zeros(kmean_ref.shape, F32)

    for g in range(MOBA_KV):
        x = buf_ref[slot, 0, :, g * HEAD_DIM:(g + 1) * HEAD_DIM]
        kmean_ref[g, pl.ds(pl.multiple_of(pg * bps, bps), bps), :] = (
            jnp.sum(x.reshape(bps, MOBA_BLK, HEAD_DIM), axis=1) * (1.0 / MOBA_BLK))

    @pl.when(pg == pl.num_programs(1) - 1)
    def _():
        own = _decode_qpos(past_len, td, R) // MOBA_BLK
        past = lax.broadcasted_iota(jnp.int32, (1, NBP), 1) < own
        for g in range(MOBA_KV):
            q3 = _stack_heads(q_ref, g, R)
            sg = lax.dot_general(q3, kmean_ref[g], NT_DIMS, precision=HI, preferred_element_type=F32)
            sg = jnp.where(past, sg, -jnp.inf)
            sel_ref[g] = ((_rank_desc(sg, past_len // MOBA_BLK, 1) < k_top) & past).astype(F32)


def _moba_dec_select(page_table, zs3, cache4, l, past_len):
    DB, TD, _ = zs3.shape
    n_pages = page_table.shape[1]
    page = cache4.shape[2]
    npg = n_pages // PAGES_PER_STEP
    NB = -(-(past_len + TD) // MOBA_BLK)
    NBP = -(-NB // LANES) * LANES
    qw = MOBA_HEADS * HEAD_DIM
    rows = PAGES_PER_STEP * page
    grid_spec = pltpu.PrefetchScalarGridSpec(
        num_scalar_prefetch=1,
        grid=(DB, npg),
        in_specs=[pl.BlockSpec((None, TD, qw), lambda b, p, pt: (b, 0, C_QB // qw)),
                  pl.BlockSpec(memory_space=pl.ANY)],
        out_specs=pl.BlockSpec((None, MOBA_KV, MOBA_REP * TD, NBP), lambda b, p, pt: (b, 0, 0, 0)),
        scratch_shapes=[pltpu.VMEM((2, 1, rows, MOBA_KV * HEAD_DIM), F32), pltpu.SemaphoreType.DMA((2,)),
                        pltpu.VMEM((MOBA_KV, NBP, HEAD_DIM), F32)])
    return pl.pallas_call(
        functools.partial(_moba_dec_select_kernel, l=l, past_len=past_len, k_top=min(MOBA_TOPK, NB)),
        grid_spec=grid_spec,
        out_shape=jax.ShapeDtypeStruct((DB, MOBA_KV, MOBA_REP * TD, NBP), F32),
        compiler_params=_cparams(("arbitrary", "arbitrary")),
        name="moba_dec_select",
    )(page_table, zs3, cache4)


def _permute_w_in(w_in):
    sizes = (768, 256, 256, 256, 256, 256, 256, 18, 768, 256, 256, 256, 256, 512, 16, 512)
    offs = np.concatenate([[0], np.cumsum(sizes)])
    (nq, nkc, nvc, nks, nvs, nkw, nvw, ngt, mq, mk, mv, gq, gk, gv, ga, gr) = [
        w_in[:, :, int(offs[i]):int(offs[i + 1])] for i in range(len(sizes))]
    L, D = w_in.shape[0], w_in.shape[1]
    z = lambda n: jnp.zeros((L, D, n), w_in.dtype)
    misc = jnp.concatenate([ngt[:, :, :9], z(7), ga, z(96), ngt[:, :, 9:], z(119)], axis=-1)
    wz = jnp.concatenate([nq, nq, nkc, nvc, nks, nvs, nkw, nvw, mq, mk, mv, gq, gv, gr, gk, misc], axis=-1)
    assert wz.shape[-1] == NZ
    return wz.astype(BF16)


def _pair_cmp_w1(w1):
    L = w1.shape[0]
    half = CMP_STRIDE // 2
    w = w1.reshape(L, 2, CMP_LEN // CMP_STRIDE, half, 2, HEAD_DIM, CMP_HID)
    w = w.transpose(0, 1, 3, 4, 5, 2, 6)
    return w.reshape(L, 2, half, 2 * HEAD_DIM, (CMP_LEN // CMP_STRIDE) * CMP_HID).astype(BF16)


def _gain_vector(nsa_g, moba_g):
    L = nsa_g.shape[0]
    gz = jnp.ones((L, NZ), F32)
    put = lambda gz, c, g, n: gz.at[:, c:c + n * HEAD_DIM].set(jnp.tile(g, (1, n)))
    gz = put(gz, C_QA, nsa_g[:, 0], NSA_HEADS)
    gz = put(gz, C_QAR, nsa_g[:, 0], NSA_HEADS)
    gz = put(gz, C_NSA + 2 * ZT, nsa_g[:, 2], NSA_KV)
    gz = put(gz, C_WIN, nsa_g[:, 3], NSA_KV)
    gz = put(gz, C_QB, moba_g[:, 0], MOBA_HEADS)
    gz = put(gz, C_MOBA, moba_g[:, 1], MOBA_KV)
    return gz.reshape(L, 1, NZ)


def _rope_tables(pos):
    half = HEAD_DIM // 2
    inv = ROPE_THETA ** (-jnp.arange(half, dtype=F32) / half)
    ang = pos.astype(F32)[:, None] * inv
    cos, sin = jnp.cos(ang), jnp.sin(ang)
    return jnp.concatenate([cos, cos], axis=-1), jnp.concatenate([-sin, sin], axis=-1)


def kernel(x_prompt, x_sample, cache_nsa, cache_moba, state_nsa_win, state_gla, page_table, c_prompt, c_sample,
           w_ada, b_ada, norm1_g, norm2_g, w_in, nsa_qk_g, nsa_cmp_pos, nsa_cmp_w1, nsa_cmp_b1, nsa_cmp_w2,
           moba_qk_g, gla_w_a2, gla_b_a, gla_norm_g, w_out, w_up, w_down):
    B, T, D = x_prompt.shape
    DB, TD, _ = x_sample.shape
    L = w_in.shape[0]
    n_pool, page = cache_nsa.shape[1], cache_nsa.shape[2]
    past_len = page_table.shape[1] * page
    assert page_table.shape[1] % PAGES_PER_STEP == 0 and (PAGES_PER_STEP * page) % MOBA_BLK == 0
    assert T % MOBA_BLK == 0 and WINDOW % MOBA_BLK == 0
    pos_p = jnp.arange(T, dtype=jnp.int32)
    pos_s = past_len + jnp.arange(TD, dtype=jnp.int32)
    MS = DB * TD

    wz = _permute_w_in(w_in)
    gz = _gain_vector(nsa_qk_g, moba_qk_g)
    wo = w_out.astype(BF16)
    wu = w_up.astype(BF16)
    wd = w_down.astype(BF16)
    wp = _pair_cmp_w1(nsa_cmp_w1)
    n1 = norm1_g.reshape(L, 1, D)
    n2 = norm2_g.reshape(L, 1, D)
    b1 = nsa_cmp_b1.reshape(L, 2, 1, CMP_HID)
    pe_flat = nsa_cmp_pos.reshape(L, 2, 1, CMP_LEN * HEAD_DIM)
    nsa_g4 = nsa_qk_g.reshape(L, 4, 1, HEAD_DIM)
    gb = gla_b_a.reshape(L, 1, GLA_HEADS * GLA_DK)
    gng = gla_norm_g.reshape(L, 1, GLA_DV)
    cos_p, sin_p = _rope_tables(pos_p)
    cos_s, sin_s = _rope_tables(jnp.tile(pos_s, DB))
    cache_nsa4 = cache_nsa.reshape(L, n_pool, page, 4 * NSA_KV * HEAD_DIM)
    cache_moba4 = cache_moba.reshape(L, n_pool, page, 2 * MOBA_KV * HEAD_DIM)
    win4 = state_nsa_win.reshape(L, DB, state_nsa_win.shape[2], 2 * NSA_KV * HEAD_DIM)

    nc = B + DB
    c_all = jnp.concatenate([c_prompt, c_sample, jnp.zeros((-nc % 8, D), F32)], axis=0)
    mod = _ada(c_all, w_ada, b_ada)

    def mods(l):
        mp = mod[l, :B].reshape(B, 1, 6, D)
        ms = jnp.repeat(mod[l, B:nc], TD, axis=0).reshape(1, MS, 6, D)
        return [mp[:, :, i] for i in range(6)], [ms[:, :, i] for i in range(6)]

    tm_p = 1024
    tpb_p = T // tm_p
    xp = x_prompt.reshape(B * T, D)
    xs = x_sample.reshape(MS, D)
    s0_p = jnp.zeros((B, GLA_HEADS, GLA_DK, GLA_DV), F32)
    outs = [[] for _ in range(8)]
    for l in range(L):
        (sh1, sc1, gt1, sh2, sc2, gt2), (sh1s, sc1s, gt1s, sh2s, sc2s, gt2s) = mods(l)
        zz = _nmm_zz(xp, sc1, sh1, n1, wz, l, gz, cos_p, sin_p, tm_p, tpb_p)
        zz3 = zz.reshape(B, T, NZ)
        kcvc = _compress_prompt(zz3, nsa_cmp_pos, nsa_cmp_w1, b1, nsa_cmp_w2, nsa_g4, l)
        oa = _nsa_prompt(zz3, kcvc)
        ob = _moba_prompt(zz3)
        og, s_fin = _gla(zz3, gla_w_a2, gb, gng, l, s0_p, 512, GLA_CHUNK)
        xp = _outproj(oa.reshape(B * T, -1), ob.reshape(B * T, -1), og.reshape(B * T, -1), wo, l, xp, gt1,
                      tm_p, tpb_p)
        u = _nmm_relu2(xp, sc2, sh2, n2, wu, l, tm_p, tpb_p)
        xp = _down(u, wd, l, xp, gt2, tm_p, tpb_p)
        outs[0].append(zz3[:, :, C_NSA:C_NSA + 4 * ZT].reshape(B, T, 4, NSA_KV, HEAD_DIM))
        outs[2].append(zz3[:, T - min(WINDOW, T):, C_WIN:C_WIN + 2 * ZT].reshape(B, -1, 2, NSA_KV, HEAD_DIM))
        outs[4].append(zz3[:, :, C_MOBA:C_MOBA + 2 * ZT].reshape(B, T, 2, MOBA_KV, HEAD_DIM))
        outs[6].append(s_fin)
        zs = _nmm_zz(xs, sc1s, sh1s, n1, wz, l, gz, cos_s, sin_s, MS, 1)
        zs3 = zs.reshape(DB, TD, NZ)
        oc_s, sel_a = _nsa_dec_select(page_table, zs3, cache_nsa4, wp, pe_flat, nsa_cmp_w1, b1, nsa_cmp_w2,
                                      nsa_g4, l, past_len)
        os_s = _paged_attn(page_table, zs3, sel_a, cache_nsa4, l, C_QAR, C_NSA + 2 * ZT, C_NSA + 3 * ZT,
                           2 * ZT, SLC_BLK, "nsa_dec_selected")
        oa_s = _nsa_dec_combine(zs3, oc_s, os_s, win4, l)
        sel_b = _moba_dec_select(page_table, zs3, cache_moba4, l, past_len)
        ob_s = _paged_attn(page_table, zs3, sel_b, cache_moba4, l, C_QB, C_MOBA, C_MOBA + ZT, 0, MOBA_BLK,
                           "moba_dec_attn").astype(BF16)
        og_s, s_fin_s = _gla(zs3, gla_w_a2, gb, gng, l, state_gla[l], TD, TD)
        xs = _outproj(oa_s.reshape(MS, -1), ob_s.reshape(MS, -1), og_s.reshape(MS, -1), wo, l, xs, gt1s, MS, 1)
        us = _nmm_relu2(xs, sc2s, sh2s, n2, wu, l, MS, 1)
        xs = _down(us, wd, l, xs, gt2s, MS, 1)
        win_rows_s = zs3[:, :, C_WIN:C_WIN + 2 * ZT].reshape(DB, TD, 2, NSA_KV, HEAD_DIM)
        win_ctx = jnp.concatenate([state_nsa_win[l], win_rows_s], axis=1)
        outs[1].append(zs3[:, :, C_NSA:C_NSA + 4 * ZT].reshape(DB, TD, 4, NSA_KV, HEAD_DIM))
        outs[3].append(win_ctx[:, -min(WINDOW, past_len + TD):])
        outs[5].append(zs3[:, :, C_MOBA:C_MOBA + 2 * ZT].reshape(DB, TD, 2, MOBA_KV, HEAD_DIM))
        outs[7].append(s_fin_s)
    st = [jnp.stack(o) for o in outs]
    return (xp.reshape(B, T, D), xs.reshape(DB, TD, D), st[0], st[1], st[2], st[3], st[4], st[5], st[6], st[7])
```

```python
import functools

import numpy as np
import jax
import jax.numpy as jnp
from jax import lax
from jax.experimental import pallas as pl
from jax.experimental.pallas import tpu as pltpu

F32 = jnp.float32
BF16 = jnp.bfloat16
HI = lax.Precision.HIGHEST

D_MODEL = 2048
HEAD_DIM = 128
NSA_HEADS = 6
NSA_KV = 2
NSA_REP = NSA_HEADS // NSA_KV
MOBA_HEADS = 6
MOBA_KV = 2
MOBA_REP = MOBA_HEADS // MOBA_KV
GLA_HEADS = 4
GLA_DK = 64
GLA_DV = 128
GLA_RANK = 16
GLA_TAU = 16.0
GLA_CHUNK = 64
CMP_LEN = 32
CMP_STRIDE = 16
CMP_HID = 128
SLC_BLK = 64
SLC_TOPN = 16
WINDOW = 512
MOBA_BLK = 256
MOBA_TOPK = 3
D_FF = 4 * D_MODEL
ROPE_THETA = 10000.0
EPS = 1e-6
SCALE = HEAD_DIM ** -0.5
NEG = -1e30
LANES = 128

ZT = 256
C_QA = 0
C_QAR = 768
C_NSA = 1536
C_WIN = 2560
C_QB = 3072
C_MOBA = 3840
C_GQ = 4352
C_GV = 4608
C_GR = 5120
C_GK = 5632
C_MISC = 5888
NZ = 6144

VMEM_LIMIT = 56 * 1024 * 1024
PAGES_PER_STEP = 16

NT_DIMS = (((1,), (1,)), ((), ()))
TN_DIMS = (((0,), (0,)), ((), ()))


def _cparams(sem):
    return pltpu.CompilerParams(dimension_semantics=sem, vmem_limit_bytes=VMEM_LIMIT)


def _ada_kernel(c_ref, w_ref, b_ref, o_ref):
    c = c_ref[...]
    a = c * jax.nn.sigmoid(c)
    o_ref[...] = jnp.dot(a, w_ref[...], precision=HI, preferred_element_type=F32) + b_ref[...]


def _ada(c_all, w_ada, b_ada):
    L, D, N = w_ada.shape
    R = c_all.shape[0]
    tn = 512
    return pl.pallas_call(
        _ada_kernel,
        grid=(L, N // tn),
        in_specs=[pl.BlockSpec((R, D), lambda l, j: (0, 0)),
                  pl.BlockSpec((None, D, tn), lambda l, j: (l, 0, j)),
                  pl.BlockSpec((None, 1, tn), lambda l, j: (l, 0, j))],
        out_specs=pl.BlockSpec((None, R, tn), lambda l, j: (l, 0, j)),
        out_shape=jax.ShapeDtypeStruct((L, R, N), F32),
        compiler_params=_cparams(("parallel", "parallel")),
        name="ada_mod",
    )(c_all, w_ada, b_ada.reshape(L, 1, N))


def _mod_norm(x_ref, sc_ref, sh_ref, gn_ref, h_ref):
    x = x_ref[...]
    y = x * lax.rsqrt(jnp.mean(x * x, axis=-1, keepdims=True) + EPS) * gn_ref[...]
    h_ref[...] = (y * (1.0 + sc_ref[...]) + sh_ref[...]).astype(BF16)


def _head_rms(a, g):
    return a * lax.rsqrt(jnp.mean(a * a, axis=-1, keepdims=True) + EPS) * g


def _nmm_zz_kernel(x_ref, sc_ref, sh_ref, gn_ref, w_ref, gz_ref, cos_ref, sin_ref, o_ref, h_ref):
    j = pl.program_id(1)

    @pl.when(j == 0)
    def _():
        _mod_norm(x_ref, sc_ref, sh_ref, gn_ref, h_ref)

    acc = jnp.dot(h_ref[...], w_ref[...], preferred_element_type=F32)
    is_norm = j < 3
    is_rope = ((j >= 3) & (j <= 5)) | (j == 8) | (j == 10) | ((j >= 12) & (j <= 15))

    @pl.when(jnp.logical_not(is_norm | is_rope))
    def _():
        o_ref[...] = acc

    @pl.when(is_norm)
    def _():
        for hh in range(ZT // HEAD_DIM):
            sl = slice(hh * HEAD_DIM, (hh + 1) * HEAD_DIM)
            o_ref[:, sl] = _head_rms(acc[:, sl], gz_ref[:, sl])

    @pl.when(is_rope)
    def _():
        for hh in range(ZT // HEAD_DIM):
            sl = slice(hh * HEAD_DIM, (hh + 1) * HEAD_DIM)
            y = _head_rms(acc[:, sl], gz_ref[:, sl])
            o_ref[:, sl] = y * cos_ref[...] + pltpu.roll(y, HEAD_DIM // 2, 1) * sin_ref[...]


def _nmm_relu2_kernel(x_ref, sc_ref, sh_ref, gn_ref, w_ref, o_ref, h_ref):
    @pl.when(pl.program_id(1) == 0)
    def _():
        _mod_norm(x_ref, sc_ref, sh_ref, gn_ref, h_ref)

    acc = jnp.dot(h_ref[...], w_ref[...], preferred_element_type=F32)
    r = jnp.maximum(acc, 0.0)
    o_ref[...] = (r * r).astype(o_ref.dtype)


def _mod_specs(sc, tpb):
    rows = sc.shape[1]
    D = sc.shape[2]
    return pl.BlockSpec((None, rows, D), lambda i, j: (i // tpb, 0, 0))


def _nmm_zz(x, sc, sh, gn, wz, l, gz, cos, sin, tm, tpb):
    M, D = x.shape
    tn = ZT
    return pl.pallas_call(
        _nmm_zz_kernel,
        grid=(M // tm, NZ // tn),
        in_specs=[pl.BlockSpec((tm, D), lambda i, j: (i, 0)),
                  _mod_specs(sc, tpb), _mod_specs(sh, tpb),
                  pl.BlockSpec((None, 1, D), lambda i, j: (l, 0, 0)),
                  pl.BlockSpec((None, D, tn), lambda i, j: (l, 0, j)),
                  pl.BlockSpec((None, 1, tn), lambda i, j: (l, 0, j)),
                  pl.BlockSpec((tm, HEAD_DIM), lambda i, j: (i % tpb, 0)),
                  pl.BlockSpec((tm, HEAD_DIM), lambda i, j: (i % tpb, 0))],
        out_specs=pl.BlockSpec((tm, tn), lambda i, j: (i, j)),
        out_shape=jax.ShapeDtypeStruct((M, NZ), F32),
        scratch_shapes=[pltpu.VMEM((tm, D), BF16)],
        compiler_params=_cparams(("parallel", "arbitrary")),
        name="in_proj",
    )(x, sc, sh, gn, wz, gz, cos, sin)


def _nmm_relu2(x, sc, sh, gn, wu, l, tm, tpb):
    M, D = x.shape
    N = wu.shape[2]
    tn = 512
    return pl.pallas_call(
        _nmm_relu2_kernel,
        grid=(M // tm, N // tn),
        in_specs=[pl.BlockSpec((tm, D), lambda i, j: (i, 0)),
                  _mod_specs(sc, tpb), _mod_specs(sh, tpb),
                  pl.BlockSpec((None, 1, D), lambda i, j: (l, 0, 0)),
                  pl.BlockSpec((None, D, tn), lambda i, j: (l, 0, j))],
        out_specs=pl.BlockSpec((tm, tn), lambda i, j: (i, j)),
        out_shape=jax.ShapeDtypeStruct((M, N), BF16),
        scratch_shapes=[pltpu.VMEM((tm, D), BF16)],
        compiler_params=_cparams(("parallel", "arbitrary")),
        name="mlp_up",
    )(x, sc, sh, gn, wu)


def _outproj_kernel(oa_ref, ob_ref, og_ref, w_ref, x_ref, gt_ref, o_ref):
    na, nb = oa_ref.shape[1], ob_ref.shape[1]
    acc = jnp.dot(oa_ref[...], w_ref[0:na, :], preferred_element_type=F32)
    acc += jnp.dot(ob_ref[...], w_ref[na:na + nb, :], preferred_element_type=F32)
    acc += jnp.dot(og_ref[...], w_ref[na + nb:, :], preferred_element_type=F32)
    o_ref[...] = x_ref[...] + gt_ref[...] * acc


def _outproj(oa, ob, og, wo, l, x, gt, tm, tpb):
    M, D = x.shape
    tn = 512
    rows = gt.shape[1]
    return pl.pallas_call(
        _outproj_kernel,
        grid=(M // tm, D // tn),
        in_specs=[pl.BlockSpec((tm, oa.shape[1]), lambda i, j: (i, 0)),
                  pl.BlockSpec((tm, ob.shape[1]), lambda i, j: (i, 0)),
                  pl.BlockSpec((tm, og.shape[1]), lambda i, j: (i, 0)),
                  pl.BlockSpec((None, D, tn), lambda i, j: (l, 0, j)),
                  pl.BlockSpec((tm, tn), lambda i, j: (i, j)),
                  pl.BlockSpec((None, rows, tn), lambda i, j: (i // tpb, 0, j))],
        out_specs=pl.BlockSpec((tm, tn), lambda i, j: (i, j)),
        out_shape=jax.ShapeDtypeStruct((M, D), F32),
        compiler_params=_cparams(("parallel", "parallel")),
        name="out_proj",
    )(oa, ob, og, wo, x, gt)


def _down_kernel(u_ref, w_ref, x_ref, gt_ref, o_ref, acc_ref):
    k = pl.program_id(2)

    @pl.when(k == 0)
    def _():
        acc_ref[...] = jnp.zeros_like(acc_ref)

    acc_ref[...] += jnp.dot(u_ref[...], w_ref[...], preferred_element_type=F32)

    @pl.when(k == pl.num_programs(2) - 1)
    def _():
        o_ref[...] = x_ref[...] + gt_ref[...] * acc_ref[...]


def _down(u, wd, l, x, gt, tm, tpb):
    M, D = x.shape
    K = u.shape[1]
    tn, tk = 512, 2048
    rows = gt.shape[1]
    return pl.pallas_call(
        _down_kernel,
        grid=(M // tm, D // tn, K // tk),
        in_specs=[pl.BlockSpec((tm, tk), lambda i, j, k: (i, k)),
                  pl.BlockSpec((None, tk, tn), lambda i, j, k: (l, k, j)),
                  pl.BlockSpec((tm, tn), lambda i, j, k: (i, j)),
                  pl.BlockSpec((None, rows, tn), lambda i, j, k: (i // tpb, 0, j))],
        out_specs=pl.BlockSpec((tm, tn), lambda i, j, k: (i, j)),
        out_shape=jax.ShapeDtypeStruct((M, D), F32),
        scratch_shapes=[pltpu.VMEM((tm, tn), F32)],
        compiler_params=_cparams(("parallel", "parallel", "arbitrary")),
        name="mlp_down",
    )(u, wd, x, gt)


def _compress_kernel(rows_ref, pe_ref, w1_ref, b1_ref, w2_ref, g_ref, o_ref, *, nch):
    slot = pl.program_id(1)
    pa = jnp.zeros((nch, CMP_HID), F32)
    pb = jnp.zeros((nch, CMP_HID), F32)
    for s in range(CMP_STRIDE):
        xs = rows_ref[pl.ds(s, nch, stride=CMP_STRIDE), :]
        wa = w1_ref[s * HEAD_DIM:(s + 1) * HEAD_DIM, :]
        wb = w1_ref[(CMP_STRIDE + s) * HEAD_DIM:(CMP_STRIDE + s + 1) * HEAD_DIM, :]
        pa += jnp.dot(xs + pe_ref[s:s + 1, :], wa, precision=HI, preferred_element_type=F32)
        pb += jnp.dot(xs + pe_ref[CMP_STRIDE + s:CMP_STRIDE + s + 1, :], wb, precision=HI,
                      preferred_element_type=F32)
    hid = b1_ref[...] + pa + pltpu.roll(pb, nch - 1, 0)
    tok = jnp.dot(jax.nn.gelu(hid), w2_ref[...], precision=HI, preferred_element_type=F32)

    @pl.when(slot == 0)
    def _():
        o_ref[...] = _head_rms(tok, g_ref[...])

    @pl.when(slot != 0)
    def _():
        o_ref[...] = tok


def _compress_prompt(zz3, pe, w1, b1, w2, gk, l):
    B, T, _ = zz3.shape
    nch = T // CMP_STRIDE
    cb = C_NSA // HEAD_DIM
    return pl.pallas_call(
        functools.partial(_compress_kernel, nch=nch),
        grid=(B, 2, NSA_KV),
        in_specs=[pl.BlockSpec((None, T, HEAD_DIM), lambda b, s, g: (b, 0, cb + 2 * s + g)),
                  pl.BlockSpec((None, None, CMP_LEN, HEAD_DIM), lambda b, s, g: (l, s, 0, 0)),
                  pl.BlockSpec((None, None, CMP_LEN * HEAD_DIM, CMP_HID), lambda b, s, g: (l, s, 0, 0)),
                  pl.BlockSpec((None, None, 1, CMP_HID), lambda b, s, g: (l, s, 0, 0)),
                  pl.BlockSpec((None, None, CMP_HID, HEAD_DIM), lambda b, s, g: (l, s, 0, 0)),
                  pl.BlockSpec((None, None, 1, HEAD_DIM), lambda b, s, g: (l, 1, 0, 0))],
        out_specs=pl.BlockSpec((None, None, None, nch, HEAD_DIM), lambda b, s, g: (s, b, g, 0, 0)),
        out_shape=jax.ShapeDtypeStruct((2, B, NSA_KV, nch, HEAD_DIM), F32),
        compiler_params=_cparams(("parallel", "parallel", "parallel")),
        name="nsa_compress",
    )(zz3, pe, w1, b1, w2, gk)


def _rank_desc(score, n, axis):
    idx = lax.broadcasted_iota(jnp.int32, score.shape, axis)
    rank = jnp.zeros(score.shape, jnp.int32)
    for i in range(n):
        ci = score[:, i:i + 1] if axis == 1 else score[i:i + 1, :]
        beats = (ci > score) | ((ci == score) & (idx > i))
        rank = rank + beats.astype(jnp.int32)
    return rank


def _online_update(carry, s, msk, v):
    m, l, acc = carry
    s = jnp.where(msk, s, NEG)
    m_new = jnp.maximum(m, jnp.max(s, axis=-1, keepdims=True))
    alpha = jnp.exp(m - m_new)
    p = jnp.where(msk, jnp.exp(s - m_new), 0.0)
    l = alpha * l + jnp.sum(p, axis=-1, keepdims=True)
    acc = alpha * acc + jnp.dot(p.astype(BF16), v, preferred_element_type=F32)
    return m_new, l, acc


def _online_init(rows):
    return (jnp.full((rows, 1), NEG, F32), jnp.zeros((rows, 1), F32), jnp.zeros((rows, HEAD_DIM), F32))


def _online_finish(carry):
    _, l, acc = carry
    return acc / jnp.maximum(l, 1e-30)


def _flash_reset(m_ref, l_ref, acc_ref):
    m_ref[...] = jnp.full(m_ref.shape, NEG, F32)
    l_ref[...] = jnp.zeros(l_ref.shape, F32)
    acc_ref[...] = jnp.zeros(acc_ref.shape, F32)


def _flash_heads(qs, k, v, biases, m_ref, l_ref, acc_ref):
    heads = range(len(qs))
    state = [(m_ref[r], l_ref[r], acc_ref[r]) for r in heads]
    new = []
    for r in heads:
        m_prev, l_prev, acc_prev = state[r]
        s = lax.dot_general(qs[r], k, NT_DIMS, preferred_element_type=F32)
        if biases[r] is not None:
            s = s + biases[r]
        m_new = jnp.maximum(m_prev, jnp.max(s, axis=-1, keepdims=True))
        alpha = jnp.exp(m_prev - m_new)
        p = jnp.exp(s - jnp.concatenate([m_new] * (s.shape[1] // LANES), axis=1))
        new.append((m_new, alpha * l_prev + jnp.sum(p, axis=-1, keepdims=True),
                    alpha * acc_prev + jnp.dot(p.astype(BF16), v, preferred_element_type=F32)))
    for r in heads:
        m_ref[r], l_ref[r], acc_ref[r] = new[r]


def _flash_result(r, l_ref, acc_ref):
    return acc_ref[r] / jnp.maximum(l_ref[r], 1e-30)


def _stack_heads(q_ref, g, rep):
    return jnp.concatenate([q_ref[:, (g * rep + r) * HEAD_DIM:(g * rep + r + 1) * HEAD_DIM] for r in range(rep)],
                           axis=0)


def _nsa_prompt_kernel(qa_ref, qar_ref, kc_ref, vc_ref, ks_ref, vs_ref, kw_ref, vw_ref, gt_ref, ovt_ref, o_ref,
                       m_ref, l_ref, acc_ref, *, tq, n_sel):
    qi = pl.program_id(2)
    R = NSA_REP
    C = kc_ref.shape[0]
    NS = ovt_ref.shape[0]
    tk = tq
    q0 = qi * tq
    qpos = q0 + lax.broadcasted_iota(jnp.int32, (tq, 1), 0)
    qpos_l = q0 + lax.broadcasted_iota(jnp.int32, (1, tq), 1)

    kc = kc_ref[...]
    vc = vc_ref[...].astype(BF16)
    cmask = (lax.broadcasted_iota(jnp.int32, (C, 1), 0) * CMP_STRIDE + (CMP_LEN - 1)) <= qpos_l
    imp = jnp.zeros((NS, tq), F32)
    o_c = []
    for r in range(R):
        q = qa_ref[:, r * HEAD_DIM:(r + 1) * HEAD_DIM]
        s = lax.dot_general(kc, q, NT_DIMS, precision=HI, preferred_element_type=F32) * SCALE
        s = jnp.where(cmask, s, NEG)
        p = jnp.where(cmask, jnp.exp(s - jnp.max(s, axis=0, keepdims=True)), 0.0)
        p = p / jnp.maximum(jnp.sum(p, axis=0, keepdims=True), 1e-30)
        o_c.append(lax.dot_general(p.astype(BF16), vc, TN_DIMS, preferred_element_type=F32))
        imp = imp + jnp.dot(ovt_ref[...], p, precision=HI, preferred_element_type=F32)

    jb = lax.broadcasted_iota(jnp.int32, (NS, 1), 0)
    cur = qpos_l // SLC_BLK
    forced = (jb == 0) | (jb == cur) | (jb == cur - 1)
    score = jnp.where(forced, jnp.inf, jnp.where(jb <= cur, imp, -jnp.inf))
    sel = (_rank_desc(score, NS, 0) < n_sel).astype(F32).T.astype(BF16)

    qs = [(qar_ref[:, r * HEAD_DIM:(r + 1) * HEAD_DIM] * SCALE).astype(BF16) for r in range(R)]

    _flash_reset(m_ref, l_ref, acc_ref)

    def sel_body(kt, _):
        k0 = pl.multiple_of(kt * tk, tk)
        k = ks_ref[pl.ds(k0, tk), :].astype(BF16)
        v = vs_ref[pl.ds(k0, tk), :].astype(BF16)
        kpos_e = k0 + lax.broadcasted_iota(jnp.int32, (NS, tk), 1)
        expand = (lax.broadcasted_iota(jnp.int32, (NS, tk), 0) == kpos_e // SLC_BLK).astype(BF16)
        selx = jnp.dot(sel, expand, preferred_element_type=F32)
        kpos = k0 + lax.broadcasted_iota(jnp.int32, (1, tk), 1)
        bias = jnp.where((selx > 0.5) & (kpos <= qpos), 0.0, NEG)
        _flash_heads(qs, k, v, [bias] * R, m_ref, l_ref, acc_ref)
        return 0

    lax.fori_loop(0, qi + 1, sel_body, 0)
    o_s = [_flash_result(r, l_ref, acc_ref) for r in range(R)]

    _flash_reset(m_ref, l_ref, acc_ref)
    for w in range(WINDOW // tk + 1):
        kt = qi - (WINDOW // tk) + w
        k0 = pl.multiple_of(jnp.maximum(kt, 0) * tk, tk)
        k = kw_ref[pl.ds(k0, tk), :].astype(BF16)
        v = vw_ref[pl.ds(k0, tk), :].astype(BF16)
        kbase = jnp.where(kt >= 0, k0, -(1 << 30))
        d = qpos - (kbase + lax.broadcasted_iota(jnp.int32, (1, tk), 1))
        bias = jnp.where((d >= 0) & (d < WINDOW), 0.0, NEG)
        _flash_heads(qs, k, v, [bias] * R, m_ref, l_ref, acc_ref)

    gate = jax.nn.sigmoid(gt_ref[...])
    for r in range(R):
        o = (gate[:, 3 * r:3 * r + 1] * o_c[r] + gate[:, 3 * r + 1:3 * r + 2] * o_s[r]
             + gate[:, 3 * r + 2:3 * r + 3] * _flash_result(r, l_ref, acc_ref))
        o_ref[:, r * HEAD_DIM:(r + 1) * HEAD_DIM] = o.astype(o_ref.dtype)


def _overlap_matrix(C, NS):
    ci, sj = np.arange(C)[:, None], np.arange(NS)[None, :]
    return ((ci * CMP_STRIDE < (sj + 1) * SLC_BLK) & (ci * CMP_STRIDE + CMP_LEN > sj * SLC_BLK)).astype(np.float32)


def _nsa_prompt(zz3, kcvc):
    B, T, _ = zz3.shape
    tq = 256
    C = kcvc.shape[3]
    NS = -(-T // SLC_BLK)
    assert NS <= LANES and T % tq == 0
    ovt = jnp.asarray(_overlap_matrix(C, NS).T)
    gw = NSA_REP * HEAD_DIM
    col = lambda c: c // HEAD_DIM
    kv_spec = lambda c: pl.BlockSpec((None, T, HEAD_DIM), lambda b, g, i: (b, 0, col(c) + g))
    return pl.pallas_call(
        functools.partial(_nsa_prompt_kernel, tq=tq, n_sel=min(SLC_TOPN, NS)),
        grid=(B, NSA_KV, T // tq),
        in_specs=[pl.BlockSpec((None, tq, gw), lambda b, g, i: (b, i, C_QA // gw + g)),
                  pl.BlockSpec((None, tq, gw), lambda b, g, i: (b, i, C_QAR // gw + g)),
                  pl.BlockSpec((None, None, None, C, HEAD_DIM), lambda b, g, i: (0, b, g, 0, 0)),
                  pl.BlockSpec((None, None, None, C, HEAD_DIM), lambda b, g, i: (1, b, g, 0, 0)),
                  kv_spec(C_NSA + 2 * ZT), kv_spec(C_NSA + 3 * ZT), kv_spec(C_WIN), kv_spec(C_WIN + ZT),
                  pl.BlockSpec((None, tq, HEAD_DIM), lambda b, g, i: (b, i, col(C_MISC) + g)),
                  pl.BlockSpec((NS, C), lambda b, g, i: (0, 0))],
        out_specs=pl.BlockSpec((None, tq, gw), lambda b, g, i: (b, i, g)),
        out_shape=jax.ShapeDtypeStruct((B, T, NSA_HEADS * HEAD_DIM), BF16),
        scratch_shapes=[pltpu.VMEM((NSA_REP, tq, LANES), F32), pltpu.VMEM((NSA_REP, tq, LANES), F32),
                        pltpu.VMEM((NSA_REP, tq, HEAD_DIM), F32)],
        compiler_params=_cparams(("parallel", "parallel", "arbitrary")),
        name="nsa_prompt",
    )(zz3, zz3, kcvc, kcvc, zz3, zz3, zz3, zz3, zz3, ovt)


def _moba_prompt_kernel(q_ref, kb_ref, vb_ref, o_ref, kmean_ref, m_ref, l_ref, acc_ref, *, tq, k_top):
    qi = pl.program_id(2)
    R = MOBA_REP
    NB = kmean_ref.shape[0]
    tk = MOBA_BLK
    own = qi

    @pl.when(qi == 0)
    def _():
        kmean_ref[...] = jnp.sum(kb_ref[...].reshape(NB, MOBA_BLK, HEAD_DIM), axis=1) * (1.0 / MOBA_BLK)

    nbi = lax.broadcasted_iota(jnp.int32, (1, NB), 1)
    past = nbi < own
    hidden = []
    for r in range(R):
        q = q_ref[:, r * HEAD_DIM:(r + 1) * HEAD_DIM]
        sg = lax.dot_general(q, kmean_ref[...], NT_DIMS, precision=HI, preferred_element_type=F32)
        sg = jnp.where(past, sg, -jnp.inf)
        hid = jnp.where((_rank_desc(sg, NB, 1) < k_top) & past, 0.0, NEG)
        hidden.append(jnp.concatenate([hid, jnp.zeros((tq, LANES - NB), F32)], axis=1).astype(BF16))
    qm = [jnp.concatenate([(q_ref[:, r * HEAD_DIM:(r + 1) * HEAD_DIM] * SCALE).astype(BF16), hidden[r]], axis=1)
          for r in range(R)]

    _flash_reset(m_ref, l_ref, acc_ref)

    def tile(n, masked, bias):
        k0 = pl.multiple_of(n * tk, tk)
        v = vb_ref[pl.ds(k0, tk), :].astype(BF16)
        onehot = jnp.where((lax.broadcasted_iota(jnp.int32, (tk, LANES), 1) == n) & masked, 1.0, 0.0).astype(BF16)
        km = jnp.concatenate([kb_ref[pl.ds(k0, tk), :].astype(BF16), onehot], axis=1)
        _flash_heads(qm, km, v, [bias] * R, m_ref, l_ref, acc_ref)

    def body(n, _):
        tile(n, True, None)
        return 0

    lax.fori_loop(0, qi, body, 0)
    causal = jnp.where(lax.broadcasted_iota(jnp.int32, (tq, tk), 1) <= lax.broadcasted_iota(jnp.int32, (tq, tk), 0),
                       0.0, NEG)
    tile(qi, False, causal)
    for r in range(R):
        o_ref[:, r * HEAD_DIM:(r + 1) * HEAD_DIM] = _flash_result(r, l_ref, acc_ref).astype(o_ref.dtype)


def _moba_prompt(zz3):
    B, T, _ = zz3.shape
    tq = MOBA_BLK
    NB = T // MOBA_BLK
    assert NB <= LANES
    gw = MOBA_REP * HEAD_DIM
    col = lambda c: c // HEAD_DIM
    return pl.pallas_call(
        functools.partial(_moba_prompt_kernel, tq=tq, k_top=min(MOBA_TOPK, NB)),
        grid=(B, MOBA_KV, T // tq),
        in_specs=[pl.BlockSpec((None, tq, gw), lambda b, g, i: (b, i, C_QB // gw + g)),
                  pl.BlockSpec((None, T, HEAD_DIM), lambda b, g, i: (b, 0, col(C_MOBA) + g)),
                  pl.BlockSpec((None, T, HEAD_DIM), lambda b, g, i: (b, 0, col(C_MOBA + ZT) + g))],
        out_specs=pl.BlockSpec((None, tq, gw), lambda b, g, i: (b, i, g)),
        out_shape=jax.ShapeDtypeStruct((B, T, MOBA_HEADS * HEAD_DIM), BF16),
        scratch_shapes=[pltpu.VMEM((NB, HEAD_DIM), F32),
                        pltpu.VMEM((MOBA_REP, tq, LANES), F32), pltpu.VMEM((MOBA_REP, tq, LANES), F32),
                        pltpu.VMEM((MOBA_REP, tq, HEAD_DIM), F32)],
        compiler_params=_cparams(("parallel", "parallel", "arbitrary")),
        name="moba_prompt",
    )(zz3, zz3, zz3)


def _gla_kernel(q_ref, k_ref, v_ref, r_ref, misc_ref, gw2_ref, gb_ref, gng_ref, s0_ref, og_ref, s_ref,
                lg_ref, st_ref, *, chunk):
    cg = pl.program_id(1)
    tt = q_ref.shape[0]

    @pl.when(cg == 0)
    def _():
        for h in range(GLA_HEADS):
            st_ref[h] = s0_ref[h].T

    ga = misc_ref[:, 16:16 + GLA_RANK]
    pre = jnp.dot(ga, gw2_ref[...], precision=HI, preferred_element_type=F32) + gb_ref[...]
    lg_ref[...] = jax.nn.log_sigmoid(pre) * (1.0 / GLA_TAU)

    ri = lax.broadcasted_iota(jnp.int32, (chunk, chunk), 0)
    ci = lax.broadcasted_iota(jnp.int32, (chunk, chunk), 1)
    causal = ci <= ri
    tri = causal.astype(F32)
    mid = chunk // 2

    def body(c, _):
        r0 = pl.multiple_of(c * chunk, chunk)
        rows = pl.ds(r0, chunk)
        b_all = jnp.dot(tri, lg_ref[rows, :], precision=HI, preferred_element_type=F32)
        q_all = q_ref[rows, :] * (GLA_DK ** -0.5)
        k_all = k_ref[rows, :]
        v_all = v_ref[rows, :]
        r_all = r_ref[rows, :]
        st_all = [st_ref[h] for h in range(GLA_HEADS)]
        st_new, o_new = [], []
        for h in range(GLA_HEADS):
            ks = slice(h * GLA_DK, (h + 1) * GLA_DK)
            vs = slice(h * GLA_DV, (h + 1) * GLA_DV)
            b = b_all[:, ks]
            bm = b[mid:mid + 1, :]
            bl = b[chunk - 1:chunk, :]
            q = q_all[:, ks]
            k = k_all[:, ks]
            vb = v_all[:, vs].astype(BF16)
            a = lax.dot_general((q * jnp.exp(b - bm)).astype(BF16), (k * jnp.exp(bm - b)).astype(BF16), NT_DIMS,
                                preferred_element_type=F32)
            a = jnp.where(causal, a, 0.0)
            st = st_all[h]
            o = lax.dot_general((q * jnp.exp(b)).astype(BF16), st.astype(BF16), NT_DIMS,
                                preferred_element_type=F32)
            o = o + jnp.dot(a.astype(BF16), vb, preferred_element_type=F32)
            kd = (k * jnp.exp(bl - b)).astype(BF16)
            st_new.append(st * jnp.exp(bl) + lax.dot_general(vb, kd, TN_DIMS, preferred_element_type=F32))
            gr = r_all[:, vs]
            o_new.append((_head_rms(o, gng_ref[...]) * (gr * jax.nn.sigmoid(gr))).astype(og_ref.dtype))
        for h in range(GLA_HEADS):
            st_ref[h] = st_new[h]
        og_ref[rows, :] = jnp.concatenate(o_new, axis=1)
        return 0

    lax.fori_loop(0, tt // chunk, body, 0)

    @pl.when(cg == pl.num_programs(1) - 1)
    def _():
        for h in range(GLA_HEADS):
            s_ref[h] = st_ref[h].T


def _gla(zz3, gw2, gb, gng, l, s0, tt, chunk):
    B, T, _ = zz3.shape
    nq = GLA_HEADS * GLA_DK
    nv = GLA_HEADS * GLA_DV
    return pl.pallas_call(
        functools.partial(_gla_kernel, chunk=chunk),
        grid=(B, T // tt),
        in_specs=[pl.BlockSpec((None, tt, nq), lambda b, c: (b, c, C_GQ // nq)),
                  pl.BlockSpec((None, tt, nq), lambda b, c: (b, c, C_GK // nq)),
                  pl.BlockSpec((None, tt, nv), lambda b, c: (b, c, C_GV // nv)),
                  pl.BlockSpec((None, tt, nv), lambda b, c: (b, c, C_GR // nv)),
                  pl.BlockSpec((None, tt, HEAD_DIM), lambda b, c: (b, c, C_MISC // HEAD_DIM)),
                  pl.BlockSpec((None, GLA_RANK, nq), lambda b, c: (l, 0, 0)),
                  pl.BlockSpec((None, 1, nq), lambda b, c: (l, 0, 0)),
                  pl.BlockSpec((None, 1, GLA_DV), lambda b, c: (l, 0, 0)),
                  pl.BlockSpec((None, GLA_HEADS, GLA_DK, GLA_DV), lambda b, c: (b, 0, 0, 0))],
        out_specs=[pl.BlockSpec((None, tt, nv), lambda b, c: (b, c, 0)),
                   pl.BlockSpec((None, GLA_HEADS, GLA_DK, GLA_DV), lambda b, c: (b, 0, 0, 0))],
        out_shape=[jax.ShapeDtypeStruct((B, T, nv), BF16),
                   jax.ShapeDtypeStruct((B, GLA_HEADS, GLA_DK, GLA_DV), F32)],
        scratch_shapes=[pltpu.VMEM((tt, nq), F32), pltpu.VMEM((GLA_HEADS, GLA_DV, GLA_DK), F32)],
        compiler_params=_cparams(("parallel", "arbitrary")),
        name="gla",
    )(zz3, zz3, zz3, zz3, zz3, gw2, gb, gng, s0)


def _page_copies(cache_ref, pt_ref, l, buf_ref, sem_ref, j0, b, pg, slot):
    page = cache_ref.shape[2]
    return [pltpu.make_async_copy(cache_ref.at[l, pt_ref[b, pg * PAGES_PER_STEP + k], :, j0 + cb, :],
                                  buf_ref.at[slot, cb, pl.ds(k * page, page), :],
                                  sem_ref.at[slot]) for k in range(PAGES_PER_STEP) for cb in range(buf_ref.shape[1])]


def _paged_fetch(cache_ref, pt_ref, l, buf_ref, sem_ref, j0):
    b, pg = pl.program_id(0), pl.program_id(1)
    npg = pl.num_programs(1)
    i = b * npg + pg
    slot = i % 2
    copies = functools.partial(_page_copies, cache_ref, pt_ref, l, buf_ref, sem_ref, j0)

    @pl.when(i == 0)
    def _():
        for c in copies(b, pg, slot):
            c.start()

    @pl.when(i + 1 < pl.num_programs(0) * npg)
    def _():
        n = i + 1
        for c in copies(n // npg, n % npg, 1 - slot):
            c.start()

    for c in copies(b, pg, slot):
        c.wait()
    return slot


def _decode_qpos(past_len, td, rep):
    t = lax.broadcasted_iota(jnp.int32, (rep * td, 1), 0) % td
    return past_len + t


def _nsa_dec_select_kernel(pt_ref, qa_ref, cache_ref, wp_ref, pe_ref, w1_ref, b1_ref, w2_ref, gk_ref, ov_ref,
                           oc_ref, sel_ref, buf_ref, sem_ref, pab_ref, *, l, past_len, n_blocks, n_sel):
    pg = pl.program_id(1)
    slot = _paged_fetch(cache_ref, pt_ref, l, buf_ref, sem_ref, 0)
    n = buf_ref.shape[2] // CMP_STRIDE
    C = pab_ref.shape[2]
    td = qa_ref.shape[0]
    R = NSA_REP
    NSP = ov_ref.shape[1]

    for kv in range(2):
        acc = jnp.zeros((NSA_KV * n, 2 * CMP_HID), F32)
        for sp in range(CMP_STRIDE // 2):
            parts = []
            for g in range(NSA_KV):
                cb = kv * NSA_KV + g
                xa = buf_ref[slot, cb, pl.ds(2 * sp, n, stride=CMP_STRIDE), :]
                xb = buf_ref[slot, cb, pl.ds(2 * sp + 1, n, stride=CMP_STRIDE), :]
                parts.append(jnp.concatenate([xa, xb], axis=1))
            x = jnp.concatenate(parts, axis=0).astype(BF16)
            acc += jnp.dot(x, wp_ref[kv, sp], preferred_element_type=F32)
        for g in range(NSA_KV):
            pab_ref[kv, g, pl.ds(pl.multiple_of(pg * n, n), n), :] = acc[g * n:(g + 1) * n]

    @pl.when(pg == pl.num_programs(1) - 1)
    def _():
        qpos = _decode_qpos(past_len, td, 1)
        qpos3 = _decode_qpos(past_len, td, R)
        cmask = (lax.broadcasted_iota(jnp.int32, (1, C), 1) * CMP_STRIDE + (CMP_LEN - 1)) <= qpos3
        toks = []
        for kv in range(2):
            pe8 = jnp.broadcast_to(pe_ref[kv], (8, CMP_LEN * HEAD_DIM))
            const = b1_ref[kv] + jnp.dot(pe8, w1_ref[kv], precision=HI, preferred_element_type=F32)[0:1]
            per_g = []
            for g in range(NSA_KV):
                pab = pab_ref[kv, g]
                hid = const + pab[:, :CMP_HID] + pltpu.roll(pab[:, CMP_HID:], C - 1, 0)
                tok = jnp.dot(jax.nn.gelu(hid), w2_ref[kv], precision=HI, preferred_element_type=F32)
                per_g.append(_head_rms(tok, gk_ref[...]) if kv == 0 else tok)
            toks.append(per_g)
        jb = lax.broadcasted_iota(jnp.int32, (1, NSP), 1)
        cur = qpos // SLC_BLK
        forced = (jb == 0) | (jb == cur) | (jb == cur - 1)
        for g in range(NSA_KV):
            kc, vc = toks[0][g], toks[1][g]
            q3 = _stack_heads(qa_ref, g, R)
            s = lax.dot_general(q3, kc, NT_DIMS, precision=HI, preferred_element_type=F32) * SCALE
            s = jnp.where(cmask, s, NEG)
            p = jnp.where(cmask, jnp.exp(s - jnp.max(s, axis=-1, keepdims=True)), 0.0)
            p = p / jnp.maximum(jnp.sum(p, axis=-1, keepdims=True), 1e-30)
            oc = jnp.dot(p.astype(BF16), vc.astype(BF16), preferred_element_type=F32)
            imp = jnp.zeros((td, NSP), F32)
            for r in range(R):
                h = g * R + r
                oc_ref[:, h * HEAD_DIM:(h + 1) * HEAD_DIM] = oc[r * td:(r + 1) * td]
                imp = imp + jnp.dot(p[r * td:(r + 1) * td], ov_ref[...], precision=HI, preferred_element_type=F32)
            score = jnp.where(forced, jnp.inf, jnp.where(jb <= cur, imp, -jnp.inf))
            sel = (_rank_desc(score, n_blocks, 1) < n_sel).astype(F32)
            sel_ref[g] = jnp.concatenate([sel] * R, axis=0)


def _nsa_dec_select(page_table, zs3, cache4, wp, pe, w1, b1, w2, gk, l, past_len):
    DB, TD, _ = zs3.shape
    n_pages = page_table.shape[1]
    page = cache4.shape[2]
    npg = n_pages // PAGES_PER_STEP
    C = past_len // CMP_STRIDE
    NS = -(-(past_len + TD) // SLC_BLK)
    NSP = -(-NS // LANES) * LANES
    ov = jnp.asarray(_overlap_matrix(C, NSP))
    qw = NSA_HEADS * HEAD_DIM
    rows = PAGES_PER_STEP * page
    grid_spec = pltpu.PrefetchScalarGridSpec(
        num_scalar_prefetch=1,
        grid=(DB, npg),
        in_specs=[pl.BlockSpec((None, TD, qw), lambda b, p, pt: (b, 0, C_QA // qw)),
                  pl.BlockSpec(memory_space=pl.ANY),
                  pl.BlockSpec((None, 2, CMP_STRIDE // 2, 2 * HEAD_DIM, 2 * CMP_HID), lambda b, p, pt: (l, 0, 0, 0, 0)),
                  pl.BlockSpec((None, 2, 1, CMP_LEN * HEAD_DIM), lambda b, p, pt: (l, 0, 0, 0)),
                  pl.BlockSpec((None, 2, CMP_LEN * HEAD_DIM, CMP_HID), lambda b, p, pt: (l, 0, 0, 0)),
                  pl.BlockSpec((None, 2, 1, CMP_HID), lambda b, p, pt: (l, 0, 0, 0)),
                  pl.BlockSpec((None, 2, CMP_HID, HEAD_DIM), lambda b, p, pt: (l, 0, 0, 0)),
                  pl.BlockSpec((None, None, 1, HEAD_DIM), lambda b, p, pt: (l, 1, 0, 0)),
                  pl.BlockSpec((C, NSP), lambda b, p, pt: (0, 0))],
        out_specs=[pl.BlockSpec((None, TD, qw), lambda b, p, pt: (b, 0, 0)),
                   pl.BlockSpec((None, NSA_KV, NSA_REP * TD, NSP), lambda b, p, pt: (b, 0, 0, 0))],
        scratch_shapes=[pltpu.VMEM((2, 2 * NSA_KV, rows, HEAD_DIM), F32), pltpu.SemaphoreType.DMA((2,)),
                        pltpu.VMEM((2, NSA_KV, C, 2 * CMP_HID), F32)])
    return pl.pallas_call(
        functools.partial(_nsa_dec_select_kernel, l=l, past_len=past_len, n_blocks=NS, n_sel=min(SLC_TOPN, NS)),
        grid_spec=grid_spec,
        out_shape=[jax.ShapeDtypeStruct((DB, TD, qw), F32),
                   jax.ShapeDtypeStruct((DB, NSA_KV, NSA_REP * TD, NSP), F32)],
        compiler_params=_cparams(("arbitrary", "arbitrary")),
        name="nsa_dec_select",
    )(page_table, zs3, cache4, wp, pe, w1, b1, w2, gk, ov)


def _paged_attn_kernel(pt_ref, q_ref, sel_ref, kn_ref, vn_ref, cache_ref, o_ref, buf_ref, sem_ref,
                       m_ref, l_ref, acc_ref, *, l, j0, blk, rep):
    pg = pl.program_id(1)
    slot = _paged_fetch(cache_ref, pt_ref, l, buf_ref, sem_ref, j0)
    tk = buf_ref.shape[2]
    td = q_ref.shape[0]
    NBP = sel_ref.shape[2]
    G = sel_ref.shape[0]

    @pl.when(pg == 0)
    def _():
        _flash_reset(m_ref, l_ref, acc_ref)

    kpos = pg * tk + lax.broadcasted_iota(jnp.int32, (NBP, tk), 1)
    expand = (lax.broadcasted_iota(jnp.int32, (NBP, tk), 0) == kpos // blk).astype(BF16)
    qs = [(_stack_heads(q_ref, g, rep) * SCALE).astype(BF16) for g in range(G)]
    for g in range(G):
        k = buf_ref[slot, g].astype(BF16)
        v = buf_ref[slot, G + g].astype(BF16)
        s = lax.dot_general(qs[g], k, NT_DIMS, preferred_element_type=F32)
        selx = jnp.dot(sel_ref[g].astype(BF16), expand, preferred_element_type=F32)
        m_ref[g], l_ref[g], acc_ref[g] = _online_update((m_ref[g], l_ref[g], acc_ref[g]), s, selx > 0.5, v)

    @pl.when(pg == pl.num_programs(1) - 1)
    def _():
        t = lax.broadcasted_iota(jnp.int32, (rep * td, 1), 0) % td
        causal = lax.broadcasted_iota(jnp.int32, (1, td), 1) <= t
        for g in range(G):
            k = kn_ref[:, g * HEAD_DIM:(g + 1) * HEAD_DIM].astype(BF16)
            v = vn_ref[:, g * HEAD_DIM:(g + 1) * HEAD_DIM].astype(BF16)
            s = lax.dot_general(qs[g], k, NT_DIMS, preferred_element_type=F32)
            o = _online_finish(_online_update((m_ref[g], l_ref[g], acc_ref[g]), s, causal, v))
            for r in range(rep):
                h = g * rep + r
                o_ref[:, h * HEAD_DIM:(h + 1) * HEAD_DIM] = o[r * td:(r + 1) * td]


def _paged_attn(page_table, zs3, sel, cache4, l, c_q, c_k, c_v, j0, blk, name):
    DB, TD, _ = zs3.shape
    G, rows_q, NBP = sel.shape[1:]
    rep = rows_q // TD
    n_pages = page_table.shape[1]
    page = cache4.shape[2]
    npg = n_pages // PAGES_PER_STEP
    qw = G * rep * HEAD_DIM
    kw = G * HEAD_DIM
    rows = PAGES_PER_STEP * page
    grid_spec = pltpu.PrefetchScalarGridSpec(
        num_scalar_prefetch=1,
        grid=(DB, npg),
        in_specs=[pl.BlockSpec((None, TD, qw), lambda b, p, pt: (b, 0, c_q // qw)),
                  pl.BlockSpec((None, G, rows_q, NBP), lambda b, p, pt: (b, 0, 0, 0)),
                  pl.BlockSpec((None, TD, kw), lambda b, p, pt: (b, 0, c_k // kw)),
                  pl.BlockSpec((None, TD, kw), lambda b, p, pt: (b, 0, c_v // kw)),
                  pl.BlockSpec(memory_space=pl.ANY)],
        out_specs=pl.BlockSpec((None, TD, qw), lambda b, p, pt: (b, 0, 0)),
        scratch_shapes=[pltpu.VMEM((2, 2 * G, rows, HEAD_DIM), F32), pltpu.SemaphoreType.DMA((2,)),
                        pltpu.VMEM((G, rows_q, 1), F32), pltpu.VMEM((G, rows_q, 1), F32),
                        pltpu.VMEM((G, rows_q, HEAD_DIM), F32)])
    return pl.pallas_call(
        functools.partial(_paged_attn_kernel, l=l, j0=j0, blk=blk, rep=rep),
        grid_spec=grid_spec,
        out_shape=jax.ShapeDtypeStruct((DB, TD, qw), F32),
        compiler_params=_cparams(("arbitrary", "arbitrary")),
        name=name,
    )(page_table, zs3, sel, zs3, zs3, cache4)


def _nsa_dec_combine_kernel(q_ref, oc_ref, os_ref, win_ref, wn_ref, gt_ref, o_ref):
    td = q_ref.shape[0]
    wb = win_ref.shape[0]
    R = NSA_REP
    t = lax.broadcasted_iota(jnp.int32, (R * td, 1), 0) % td
    d_buf = wb + t - lax.broadcasted_iota(jnp.int32, (1, wb), 1)
    d_new = t - lax.broadcasted_iota(jnp.int32, (1, td), 1)
    for g in range(NSA_KV):
        ksl = slice(g * HEAD_DIM, (g + 1) * HEAD_DIM)
        vsl = slice((NSA_KV + g) * HEAD_DIM, (NSA_KV + g + 1) * HEAD_DIM)
        q3 = (_stack_heads(q_ref, g, R) * SCALE).astype(BF16)
        carry = _online_init(R * td)
        s = lax.dot_general(q3, win_ref[:, ksl].astype(BF16), NT_DIMS, preferred_element_type=F32)
        carry = _online_update(carry, s, (d_buf >= 0) & (d_buf < WINDOW), win_ref[:, vsl].astype(BF16))
        s = lax.dot_general(q3, wn_ref[:, ksl].astype(BF16), NT_DIMS, preferred_element_type=F32)
        carry = _online_update(carry, s, (d_new >= 0) & (d_new < WINDOW), wn_ref[:, vsl].astype(BF16))
        o_w = _online_finish(carry)
        gate = jax.nn.sigmoid(gt_ref[:, g * LANES:(g + 1) * LANES])
        for r in range(R):
            cols = slice((g * R + r) * HEAD_DIM, (g * R + r + 1) * HEAD_DIM)
            o = (gate[:, 3 * r:3 * r + 1] * oc_ref[:, cols] + gate[:, 3 * r + 1:3 * r + 2] * os_ref[:, cols]
                 + gate[:, 3 * r + 2:3 * r + 3] * o_w[r * td:(r + 1) * td])
            o_ref[:, cols] = o.astype(o_ref.dtype)


def _nsa_dec_combine(zs3, oc, o_s, win4, l):
    DB, TD, _ = zs3.shape
    qw = NSA_HEADS * HEAD_DIM
    wb, ww = win4.shape[2], win4.shape[3]
    return pl.pallas_call(
        _nsa_dec_combine_kernel,
        grid=(DB,),
        in_specs=[pl.BlockSpec((None, TD, qw), lambda b: (b, 0, C_QAR // qw)),
                  pl.BlockSpec((None, TD, qw), lambda b: (b, 0, 0)),
                  pl.BlockSpec((None, TD, qw), lambda b: (b, 0, 0)),
                  pl.BlockSpec((None, None, wb, ww), lambda b: (l, b, 0, 0)),
                  pl.BlockSpec((None, TD, ww), lambda b: (b, 0, C_WIN // ww)),
                  pl.BlockSpec((None, TD, ZT), lambda b: (b, 0, C_MISC // ZT))],
        out_specs=pl.BlockSpec((None, TD, qw), lambda b: (b, 0, 0)),
        out_shape=jax.ShapeDtypeStruct((DB, TD, qw), BF16),
        compiler_params=_cparams(("parallel",)),
        name="nsa_dec_combine",
    )(zs3, oc, o_s, win4, zs3, zs3)


def _moba_dec_select_kernel(pt_ref, q_ref, cache_ref, sel_ref, buf_ref, sem_ref, kmean_ref,
                            *, l, past_len, k_top):
    pg = pl.program_id(1)
    slot = _paged_fetch(cache_ref, pt_ref, l, buf_ref, sem_ref, 0)
    bps = buf_ref.shape[2] // MOBA_BLK
    td = q_ref.shape[0]
    NBP = kmean_ref.shape[1]
    R = MOBA_REP

    @pl.when(pg == 0)
    def _():
        kmean_ref[...] = jnp.zeros(kmean_ref.shape, F32)

    for g in range(MOBA_KV):
        x = buf_ref[slot, g]
        kmean_ref[g, pl.ds(pl.multiple_of(pg * bps, bps), bps), :] = (
            jnp.sum(x.reshape(bps, MOBA_BLK, HEAD_DIM), axis=1) * (1.0 / MOBA_BLK))

    @pl.when(pg == pl.num_programs(1) - 1)
    def _():
        own = _decode_qpos(past_len, td, R) // MOBA_BLK
        past = lax.broadcasted_iota(jnp.int32, (1, NBP), 1) < own
        for g in range(MOBA_KV):
            q3 = _stack_heads(q_ref, g, R)
            sg = lax.dot_general(q3, kmean_ref[g], NT_DIMS, precision=HI, preferred_element_type=F32)
            sg = jnp.where(past, sg, -jnp.inf)
            sel_ref[g] = ((_rank_desc(sg, past_len // MOBA_BLK, 1) < k_top) & past).astype(F32)


def _moba_dec_select(page_table, zs3, cache4, l, past_len):
    DB, TD, _ = zs3.shape
    n_pages = page_table.shape[1]
    page = cache4.shape[2]
    npg = n_pages // PAGES_PER_STEP
    NB = -(-(past_len + TD) // MOBA_BLK)
    NBP = -(-NB // LANES) * LANES
    qw = MOBA_HEADS * HEAD_DIM
    rows = PAGES_PER_STEP * page
    grid_spec = pltpu.PrefetchScalarGridSpec(
        num_scalar_prefetch=1,
        grid=(DB, npg),
        in_specs=[pl.BlockSpec((None, TD, qw), lambda b, p, pt: (b, 0, C_QB // qw)),
                  pl.BlockSpec(memory_space=pl.ANY)],
        out_specs=pl.BlockSpec((None, MOBA_KV, MOBA_REP * TD, NBP), lambda b, p, pt: (b, 0, 0, 0)),
        scratch_shapes=[pltpu.VMEM((2, MOBA_KV, rows, HEAD_DIM), F32), pltpu.SemaphoreType.DMA((2,)),
                        pltpu.VMEM((MOBA_KV, NBP, HEAD_DIM), F32)])
    return pl.pallas_call(
        functools.partial(_moba_dec_select_kernel, l=l, past_len=past_len, k_top=min(MOBA_TOPK, NB)),
        grid_spec=grid_spec,
        out_shape=jax.ShapeDtypeStruct((DB, MOBA_KV, MOBA_REP * TD, NBP), F32),
        compiler_params=_cparams(("arbitrary", "arbitrary")),
        name="moba_dec_select",
    )(page_table, zs3, cache4)


def _permute_w_in(w_in):
    sizes = (768, 256, 256, 256, 256, 256, 256, 18, 768, 256, 256, 256, 256, 512, 16, 512)
    offs = np.concatenate([[0], np.cumsum(sizes)])
    (nq, nkc, nvc, nks, nvs, nkw, nvw, ngt, mq, mk, mv, gq, gk, gv, ga, gr) = [
        w_in[:, :, int(offs[i]):int(offs[i + 1])] for i in range(len(sizes))]
    L, D = w_in.shape[0], w_in.shape[1]
    z = lambda n: jnp.zeros((L, D, n), w_in.dtype)
    misc = jnp.concatenate([ngt[:, :, :9], z(7), ga, z(96), ngt[:, :, 9:], z(119)], axis=-1)
    wz = jnp.concatenate([nq, nq, nkc, nvc, nks, nvs, nkw, nvw, mq, mk, mv, gq, gv, gr, gk, misc], axis=-1)
    assert wz.shape[-1] == NZ
    return wz.astype(BF16)


def _pair_cmp_w1(w1):
    L = w1.shape[0]
    half = CMP_STRIDE // 2
    w = w1.reshape(L, 2, CMP_LEN // CMP_STRIDE, half, 2, HEAD_DIM, CMP_HID)
    w = w.transpose(0, 1, 3, 4, 5, 2, 6)
    return w.reshape(L, 2, half, 2 * HEAD_DIM, (CMP_LEN // CMP_STRIDE) * CMP_HID).astype(BF16)


def _gain_vector(nsa_g, moba_g):
    L = nsa_g.shape[0]
    gz = jnp.ones((L, NZ), F32)
    put = lambda gz, c, g, n: gz.at[:, c:c + n * HEAD_DIM].set(jnp.tile(g, (1, n)))
    gz = put(gz, C_QA, nsa_g[:, 0], NSA_HEADS)
    gz = put(gz, C_QAR, nsa_g[:, 0], NSA_HEADS)
    gz = put(gz, C_NSA + 2 * ZT, nsa_g[:, 2], NSA_KV)
    gz = put(gz, C_WIN, nsa_g[:, 3], NSA_KV)
    gz = put(gz, C_QB, moba_g[:, 0], MOBA_HEADS)
    gz = put(gz, C_MOBA, moba_g[:, 1], MOBA_KV)
    return gz.reshape(L, 1, NZ)


def _rope_tables(pos):
    half = HEAD_DIM // 2
    inv = ROPE_THETA ** (-jnp.arange(half, dtype=F32) / half)
    ang = pos.astype(F32)[:, None] * inv
    cos, sin = jnp.cos(ang), jnp.sin(ang)
    return jnp.concatenate([cos, cos], axis=-1), jnp.concatenate([-sin, sin], axis=-1)


def kernel(x_prompt, x_sample, cache_nsa, cache_moba, state_nsa_win, state_gla, page_table, c_prompt, c_sample,
           w_ada, b_ada, norm1_g, norm2_g, w_in, nsa_qk_g, nsa_cmp_pos, nsa_cmp_w1, nsa_cmp_b1, nsa_cmp_w2,
           moba_qk_g, gla_w_a2, gla_b_a, gla_norm_g, w_out, w_up, w_down):
    B, T, D = x_prompt.shape
    DB, TD, _ = x_sample.shape
    L = w_in.shape[0]
    n_pool, page = cache_nsa.shape[1], cache_nsa.shape[2]
    past_len = page_table.shape[1] * page
    assert page_table.shape[1] % PAGES_PER_STEP == 0 and (PAGES_PER_STEP * page) % MOBA_BLK == 0
    assert T % MOBA_BLK == 0 and WINDOW % MOBA_BLK == 0
    pos_p = jnp.arange(T, dtype=jnp.int32)
    pos_s = past_len + jnp.arange(TD, dtype=jnp.int32)
    MS = DB * TD

    wz = _permute_w_in(w_in)
    gz = _gain_vector(nsa_qk_g, moba_qk_g)
    wo = w_out.astype(BF16)
    wu = w_up.astype(BF16)
    wd = w_down.astype(BF16)
    wp = _pair_cmp_w1(nsa_cmp_w1)
    n1 = norm1_g.reshape(L, 1, D)
    n2 = norm2_g.reshape(L, 1, D)
    b1 = nsa_cmp_b1.reshape(L, 2, 1, CMP_HID)
    pe_flat = nsa_cmp_pos.reshape(L, 2, 1, CMP_LEN * HEAD_DIM)
    nsa_g4 = nsa_qk_g.reshape(L, 4, 1, HEAD_DIM)
    gb = gla_b_a.reshape(L, 1, GLA_HEADS * GLA_DK)
    gng = gla_norm_g.reshape(L, 1, GLA_DV)
    cos_p, sin_p = _rope_tables(pos_p)
    cos_s, sin_s = _rope_tables(jnp.tile(pos_s, DB))
    cache_nsa4 = cache_nsa.reshape(L, n_pool, page, 4 * NSA_KV, HEAD_DIM)
    cache_moba4 = cache_moba.reshape(L, n_pool, page, 2 * MOBA_KV, HEAD_DIM)
    win4 = state_nsa_win.reshape(L, DB, state_nsa_win.shape[2], 2 * NSA_KV * HEAD_DIM)

    nc = B + DB
    c_all = jnp.concatenate([c_prompt, c_sample, jnp.zeros((-nc % 8, D), F32)], axis=0)
    mod = _ada(c_all, w_ada, b_ada)

    def mods(l):
        mp = mod[l, :B].reshape(B, 1, 6, D)
        ms = jnp.repeat(mod[l, B:nc], TD, axis=0).reshape(1, MS, 6, D)
        return [mp[:, :, i] for i in range(6)], [ms[:, :, i] for i in range(6)]

    tm_p = 1024
    tpb_p = T // tm_p
    xp = x_prompt.reshape(B * T, D)
    xs = x_sample.reshape(MS, D)
    s0_p = jnp.zeros((B, GLA_HEADS, GLA_DK, GLA_DV), F32)
    outs = [[] for _ in range(8)]
    for l in range(L):
        (sh1, sc1, gt1, sh2, sc2, gt2), (sh1s, sc1s, gt1s, sh2s, sc2s, gt2s) = mods(l)
        zz = _nmm_zz(xp, sc1, sh1, n1, wz, l, gz, cos_p, sin_p, tm_p, tpb_p)
        zz3 = zz.reshape(B, T, NZ)
        kcvc = _compress_prompt(zz3, nsa_cmp_pos, nsa_cmp_w1, b1, nsa_cmp_w2, nsa_g4, l)
        oa = _nsa_prompt(zz3, kcvc)
        ob = _moba_prompt(zz3)
        og, s_fin = _gla(zz3, gla_w_a2, gb, gng, l, s0_p, 512, GLA_CHUNK)
        xp = _outproj(oa.reshape(B * T, -1), ob.reshape(B * T, -1), og.reshape(B * T, -1), wo, l, xp, gt1,
                      tm_p, tpb_p)
        u = _nmm_relu2(xp, sc2, sh2, n2, wu, l, tm_p, tpb_p)
        xp = _down(u, wd, l, xp, gt2, tm_p, tpb_p)
        outs[0].append(zz3[:, :, C_NSA:C_NSA + 4 * ZT].reshape(B, T, 4, NSA_KV, HEAD_DIM))
        outs[2].append(zz3[:, T - min(WINDOW, T):, C_WIN:C_WIN + 2 * ZT].reshape(B, -1, 2, NSA_KV, HEAD_DIM))
        outs[4].append(zz3[:, :, C_MOBA:C_MOBA + 2 * ZT].reshape(B, T, 2, MOBA_KV, HEAD_DIM))
        outs[6].append(s_fin)
        zs = _nmm_zz(xs, sc1s, sh1s, n1, wz, l, gz, cos_s, sin_s, MS, 1)
        zs3 = zs.reshape(DB, TD, NZ)
        oc_s, sel_a = _nsa_dec_select(page_table, zs3, cache_nsa4, wp, pe_flat, nsa_cmp_w1, b1, nsa_cmp_w2,
                                      nsa_g4, l, past_len)
        os_s = _paged_attn(page_table, zs3, sel_a, cache_nsa4, l, C_QAR, C_NSA + 2 * ZT, C_NSA + 3 * ZT,
                           2 * NSA_KV, SLC_BLK, "nsa_dec_selected")
        oa_s = _nsa_dec_combine(zs3, oc_s, os_s, win4, l)
        sel_b = _moba_dec_select(page_table, zs3, cache_moba4, l, past_len)
        ob_s = _paged_attn(page_table, zs3, sel_b, cache_moba4, l, C_QB, C_MOBA, C_MOBA + ZT, 0, MOBA_BLK,
                           "moba_dec_attn").astype(BF16)
        og_s, s_fin_s = _gla(zs3, gla_w_a2, gb, gng, l, state_gla[l], TD, TD)
        xs = _outproj(oa_s.reshape(MS, -1), ob_s.reshape(MS, -1), og_s.reshape(MS, -1), wo, l, xs, gt1s, MS, 1)
        us = _nmm_relu2(xs, sc2s, sh2s, n2, wu, l, MS, 1)
        xs = _down(us, wd, l, xs, gt2s, MS, 1)
        win_rows_s = zs3[:, :, C_WIN:C_WIN + 2 * ZT].reshape(DB, TD, 2, NSA_KV, HEAD_DIM)
        win_ctx = jnp.concatenate([state_nsa_win[l], win_rows_s], axis=1)
        outs[1].append(zs3[:, :, C_NSA:C_NSA + 4 * ZT].reshape(DB, TD, 4, NSA_KV, HEAD_DIM))
        outs[3].append(win_ctx[:, -min(WINDOW, past_len + TD):])
        outs[5].append(zs3[:, :, C_MOBA:C_MOBA + 2 * ZT].reshape(DB, TD, 2, MOBA_KV, HEAD_DIM))
        outs[7].append(s_fin_s)
    st = [jnp.stack(o) for o in outs]
    return (xp.reshape(B, T, D), xs.reshape(DB, TD, D), st[0], st[1], st[2], st[3], st[4], st[5], st[6], st[7])
```

```python
import functools

import numpy as np
import jax
import jax.numpy as jnp
from jax import lax
from jax.experimental import pallas as pl
from jax.experimental.pallas import tpu as pltpu

F32 = jnp.float32
BF16 = jnp.bfloat16
HI = lax.Precision.HIGHEST

D_MODEL = 2048
HEAD_DIM = 128
NSA_HEADS = 6
NSA_KV = 2
NSA_REP = NSA_HEADS // NSA_KV
MOBA_HEADS = 6
MOBA_KV = 2
MOBA_REP = MOBA_HEADS // MOBA_KV
GLA_HEADS = 4
GLA_DK = 64
GLA_DV = 128
GLA_RANK = 16
GLA_TAU = 16.0
GLA_CHUNK = 64
CMP_LEN = 32
CMP_STRIDE = 16
CMP_HID = 128
SLC_BLK = 64
SLC_TOPN = 16
WINDOW = 512
MOBA_BLK = 256
MOBA_TOPK = 3
D_FF = 4 * D_MODEL
ROPE_THETA = 10000.0
EPS = 1e-6
SCALE = HEAD_DIM ** -0.5
NEG = -1e30
LANES = 128

ZT = 256
C_QA = 0
C_QAR = 768
C_NSA = 1536
C_WIN = 2560
C_QB = 3072
C_MOBA = 3840
C_GQ = 4352
C_GV = 4608
C_GR = 5120
C_GK = 5632
C_MISC = 5888
NZ = 6144

VMEM_LIMIT = 56 * 1024 * 1024
PAGES_PER_STEP = 16

NT_DIMS = (((1,), (1,)), ((), ()))
TN_DIMS = (((0,), (0,)), ((), ()))


def _cparams(sem):
    return pltpu.CompilerParams(dimension_semantics=sem, vmem_limit_bytes=VMEM_LIMIT)


def _ada_kernel(c_ref, w_ref, b_ref, o_ref):
    c = c_ref[...]
    a = c * jax.nn.sigmoid(c)
    o_ref[...] = jnp.dot(a, w_ref[...], precision=HI, preferred_element_type=F32) + b_ref[...]


def _ada(c_all, w_ada, b_ada):
    L, D, N = w_ada.shape
    R = c_all.shape[0]
    tn = 512
    return pl.pallas_call(
        _ada_kernel,
        grid=(L, N // tn),
        in_specs=[pl.BlockSpec((R, D), lambda l, j: (0, 0)),
                  pl.BlockSpec((None, D, tn), lambda l, j: (l, 0, j)),
                  pl.BlockSpec((None, 1, tn), lambda l, j: (l, 0, j))],
        out_specs=pl.BlockSpec((None, R, tn), lambda l, j: (l, 0, j)),
        out_shape=jax.ShapeDtypeStruct((L, R, N), F32),
        compiler_params=_cparams(("parallel", "parallel")),
        name="ada_mod",
    )(c_all, w_ada, b_ada.reshape(L, 1, N))


def _mod_norm(x_ref, sc_ref, sh_ref, gn_ref, h_ref):
    x = x_ref[...]
    y = x * lax.rsqrt(jnp.mean(x * x, axis=-1, keepdims=True) + EPS) * gn_ref[...]
    h_ref[...] = (y * (1.0 + sc_ref[...]) + sh_ref[...]).astype(BF16)


def _head_rms(a, g):
    return a * lax.rsqrt(jnp.mean(a * a, axis=-1, keepdims=True) + EPS) * g


def _nmm_zz_kernel(x_ref, sc_ref, sh_ref, gn_ref, w_ref, gz_ref, cos_ref, sin_ref, o_ref, h_ref):
    j = pl.program_id(1)

    @pl.when(j == 0)
    def _():
        _mod_norm(x_ref, sc_ref, sh_ref, gn_ref, h_ref)

    acc = jnp.dot(h_ref[...], w_ref[...], preferred_element_type=F32)
    for t in range(o_ref.shape[1] // ZT):
        jt = j * (o_ref.shape[1] // ZT) + t
        is_norm = jt < 3
        is_rope = ((jt >= 3) & (jt <= 5)) | (jt == 8) | (jt == 10) | ((jt >= 12) & (jt <= 15))
        heads = [slice(t * ZT + hh * HEAD_DIM, t * ZT + (hh + 1) * HEAD_DIM) for hh in range(ZT // HEAD_DIM)]

        @pl.when(jnp.logical_not(is_norm | is_rope))
        def _():
            o_ref[:, t * ZT:(t + 1) * ZT] = acc[:, t * ZT:(t + 1) * ZT]

        @pl.when(is_norm)
        def _():
            for sl in heads:
                o_ref[:, sl] = _head_rms(acc[:, sl], gz_ref[:, sl])

        @pl.when(is_rope)
        def _():
            for sl in heads:
                y = _head_rms(acc[:, sl], gz_ref[:, sl])
                o_ref[:, sl] = y * cos_ref[...] + pltpu.roll(y, HEAD_DIM // 2, 1) * sin_ref[...]


def _nmm_relu2_kernel(x_ref, sc_ref, sh_ref, gn_ref, w_ref, o_ref, h_ref):
    @pl.when(pl.program_id(1) == 0)
    def _():
        _mod_norm(x_ref, sc_ref, sh_ref, gn_ref, h_ref)

    acc = jnp.dot(h_ref[...], w_ref[...], preferred_element_type=F32)
    r = jnp.maximum(acc, 0.0)
    o_ref[...] = (r * r).astype(o_ref.dtype)


def _mod_specs(sc, tpb):
    rows = sc.shape[1]
    D = sc.shape[2]
    return pl.BlockSpec((None, rows, D), lambda i, j: (i // tpb, 0, 0))


def _nmm_zz(x, sc, sh, gn, wz, l, gz, cos, sin, tm, tpb):
    M, D = x.shape
    tn = 2 * ZT
    return pl.pallas_call(
        _nmm_zz_kernel,
        grid=(M // tm, NZ // tn),
        in_specs=[pl.BlockSpec((tm, D), lambda i, j: (i, 0)),
                  _mod_specs(sc, tpb), _mod_specs(sh, tpb),
                  pl.BlockSpec((None, 1, D), lambda i, j: (l, 0, 0)),
                  pl.BlockSpec((None, D, tn), lambda i, j: (l, 0, j)),
                  pl.BlockSpec((None, 1, tn), lambda i, j: (l, 0, j)),
                  pl.BlockSpec((tm, HEAD_DIM), lambda i, j: (i % tpb, 0)),
                  pl.BlockSpec((tm, HEAD_DIM), lambda i, j: (i % tpb, 0))],
        out_specs=pl.BlockSpec((tm, tn), lambda i, j: (i, j)),
        out_shape=jax.ShapeDtypeStruct((M, NZ), F32),
        scratch_shapes=[pltpu.VMEM((tm, D), BF16)],
        compiler_params=_cparams(("parallel", "arbitrary")),
        name="in_proj",
    )(x, sc, sh, gn, wz, gz, cos, sin)


def _nmm_relu2(x, sc, sh, gn, wu, l, tm, tpb):
    M, D = x.shape
    N = wu.shape[2]
    tn = 512
    return pl.pallas_call(
        _nmm_relu2_kernel,
        grid=(M // tm, N // tn),
        in_specs=[pl.BlockSpec((tm, D), lambda i, j: (i, 0)),
                  _mod_specs(sc, tpb), _mod_specs(sh, tpb),
                  pl.BlockSpec((None, 1, D), lambda i, j: (l, 0, 0)),
                  pl.BlockSpec((None, D, tn), lambda i, j: (l, 0, j))],
        out_specs=pl.BlockSpec((tm, tn), lambda i, j: (i, j)),
        out_shape=jax.ShapeDtypeStruct((M, N), BF16),
        scratch_shapes=[pltpu.VMEM((tm, D), BF16)],
        compiler_params=_cparams(("parallel", "arbitrary")),
        name="mlp_up",
    )(x, sc, sh, gn, wu)


def _outproj_kernel(oa_ref, ob_ref, og_ref, w_ref, x_ref, gt_ref, o_ref):
    na, nb = oa_ref.shape[1], ob_ref.shape[1]
    acc = jnp.dot(oa_ref[...], w_ref[0:na, :], preferred_element_type=F32)
    acc += jnp.dot(ob_ref[...], w_ref[na:na + nb, :], preferred_element_type=F32)
    acc += jnp.dot(og_ref[...], w_ref[na + nb:, :], preferred_element_type=F32)
    o_ref[...] = x_ref[...] + gt_ref[...] * acc


def _outproj(oa, ob, og, wo, l, x, gt, tm, tpb):
    M, D = x.shape
    tn = 512
    rows = gt.shape[1]
    return pl.pallas_call(
        _outproj_kernel,
        grid=(M // tm, D // tn),
        in_specs=[pl.BlockSpec((tm, oa.shape[1]), lambda i, j: (i, 0)),
                  pl.BlockSpec((tm, ob.shape[1]), lambda i, j: (i, 0)),
                  pl.BlockSpec((tm, og.shape[1]), lambda i, j: (i, 0)),
                  pl.BlockSpec((None, D, tn), lambda i, j: (l, 0, j)),
                  pl.BlockSpec((tm, tn), lambda i, j: (i, j)),
                  pl.BlockSpec((None, rows, tn), lambda i, j: (i // tpb, 0, j))],
        out_specs=pl.BlockSpec((tm, tn), lambda i, j: (i, j)),
        out_shape=jax.ShapeDtypeStruct((M, D), F32),
        compiler_params=_cparams(("parallel", "parallel")),
        name="out_proj",
    )(oa, ob, og, wo, x, gt)


def _down_kernel(u_ref, w_ref, x_ref, gt_ref, o_ref, acc_ref):
    k = pl.program_id(2)

    @pl.when(k == 0)
    def _():
        acc_ref[...] = jnp.zeros_like(acc_ref)

    acc_ref[...] += jnp.dot(u_ref[...], w_ref[...], preferred_element_type=F32)

    @pl.when(k == pl.num_programs(2) - 1)
    def _():
        o_ref[...] = x_ref[...] + gt_ref[...] * acc_ref[...]


def _down(u, wd, l, x, gt, tm, tpb):
    M, D = x.shape
    K = u.shape[1]
    tn, tk = 512, 2048
    rows = gt.shape[1]
    return pl.pallas_call(
        _down_kernel,
        grid=(M // tm, D // tn, K // tk),
        in_specs=[pl.BlockSpec((tm, tk), lambda i, j, k: (i, k)),
                  pl.BlockSpec((None, tk, tn), lambda i, j, k: (l, k, j)),
                  pl.BlockSpec((tm, tn), lambda i, j, k: (i, j)),
                  pl.BlockSpec((None, rows, tn), lambda i, j, k: (i // tpb, 0, j))],
        out_specs=pl.BlockSpec((tm, tn), lambda i, j, k: (i, j)),
        out_shape=jax.ShapeDtypeStruct((M, D), F32),
        scratch_shapes=[pltpu.VMEM((tm, tn), F32)],
        compiler_params=_cparams(("parallel", "parallel", "arbitrary")),
        name="mlp_down",
    )(u, wd, x, gt)


def _compress_kernel(rows_ref, pe_ref, w1_ref, b1_ref, w2_ref, g_ref, o_ref, *, nch):
    slot = pl.program_id(1)
    pa = jnp.zeros((nch, CMP_HID), F32)
    pb = jnp.zeros((nch, CMP_HID), F32)
    for s in range(CMP_STRIDE):
        xs = rows_ref[pl.ds(s, nch, stride=CMP_STRIDE), :]
        wa = w1_ref[s * HEAD_DIM:(s + 1) * HEAD_DIM, :]
        wb = w1_ref[(CMP_STRIDE + s) * HEAD_DIM:(CMP_STRIDE + s + 1) * HEAD_DIM, :]
        pa += jnp.dot(xs + pe_ref[s:s + 1, :], wa, precision=HI, preferred_element_type=F32)
        pb += jnp.dot(xs + pe_ref[CMP_STRIDE + s:CMP_STRIDE + s + 1, :], wb, precision=HI,
                      preferred_element_type=F32)
    hid = b1_ref[...] + pa + pltpu.roll(pb, nch - 1, 0)
    tok = jnp.dot(jax.nn.gelu(hid), w2_ref[...], precision=HI, preferred_element_type=F32)

    @pl.when(slot == 0)
    def _():
        o_ref[...] = _head_rms(tok, g_ref[...])

    @pl.when(slot != 0)
    def _():
        o_ref[...] = tok


def _compress_prompt(zz3, pe, w1, b1, w2, gk, l):
    B, T, _ = zz3.shape
    nch = T // CMP_STRIDE
    cb = C_NSA // HEAD_DIM
    return pl.pallas_call(
        functools.partial(_compress_kernel, nch=nch),
        grid=(B, 2, NSA_KV),
        in_specs=[pl.BlockSpec((None, T, HEAD_DIM), lambda b, s, g: (b, 0, cb + 2 * s + g)),
                  pl.BlockSpec((None, None, CMP_LEN, HEAD_DIM), lambda b, s, g: (l, s, 0, 0)),
                  pl.BlockSpec((None, None, CMP_LEN * HEAD_DIM, CMP_HID), lambda b, s, g: (l, s, 0, 0)),
                  pl.BlockSpec((None, None, 1, CMP_HID), lambda b, s, g: (l, s, 0, 0)),
                  pl.BlockSpec((None, None, CMP_HID, HEAD_DIM), lambda b, s, g: (l, s, 0, 0)),
                  pl.BlockSpec((None, None, 1, HEAD_DIM), lambda b, s, g: (l, 1, 0, 0))],
        out_specs=pl.BlockSpec((None, None, None, nch, HEAD_DIM), lambda b, s, g: (s, b, g, 0, 0)),
        out_shape=jax.ShapeDtypeStruct((2, B, NSA_KV, nch, HEAD_DIM), F32),
        compiler_params=_cparams(("parallel", "parallel", "parallel")),
        name="nsa_compress",
    )(zz3, pe, w1, b1, w2, gk)


def _rank_desc(score, n, axis):
    idx = lax.broadcasted_iota(jnp.int32, score.shape, axis)
    rank = jnp.zeros(score.shape, jnp.int32)
    for i in range(n):
        ci = score[:, i:i + 1] if axis == 1 else score[i:i + 1, :]
        beats = (ci > score) | ((ci == score) & (idx > i))
        rank = rank + beats.astype(jnp.int32)
    return rank


def _online_update(carry, s, msk, v):
    m, l, acc = carry
    s = jnp.where(msk, s, NEG)
    m_new = jnp.maximum(m, jnp.max(s, axis=-1, keepdims=True))
    alpha = jnp.exp(m - m_new)
    p = jnp.where(msk, jnp.exp(s - m_new), 0.0)
    l = alpha * l + jnp.sum(p, axis=-1, keepdims=True)
    acc = alpha * acc + jnp.dot(p.astype(BF16), v, preferred_element_type=F32)
    return m_new, l, acc


def _online_init(rows):
    return (jnp.full((rows, 1), NEG, F32), jnp.zeros((rows, 1), F32), jnp.zeros((rows, HEAD_DIM), F32))


def _online_finish(carry):
    _, l, acc = carry
    return acc / jnp.maximum(l, 1e-30)


def _flash_reset(m_ref, l_ref, acc_ref):
    m_ref[...] = jnp.full(m_ref.shape, NEG, F32)
    l_ref[...] = jnp.zeros(l_ref.shape, F32)
    acc_ref[...] = jnp.zeros(acc_ref.shape, F32)


def _flash_heads(qs, k, v, biases, m_ref, l_ref, acc_ref):
    heads = range(len(qs))
    state = [(m_ref[r], l_ref[r], acc_ref[r]) for r in heads]
    v1 = jnp.concatenate([v, jnp.ones_like(v)], axis=1)
    new = []
    for r in heads:
        m_prev, l_prev, acc_prev = state[r]
        s = lax.dot_general(qs[r], k, NT_DIMS, preferred_element_type=F32)
        if biases[r] is not None:
            s = s + biases[r]
        m_new = jnp.maximum(m_prev, jnp.max(s, axis=-1, keepdims=True))
        alpha = jnp.exp(m_prev - m_new)
        p = jnp.exp(s - jnp.concatenate([m_new] * (s.shape[1] // LANES), axis=1))
        pv = jnp.dot(p.astype(BF16), v1, preferred_element_type=F32)
        new.append((m_new, alpha * l_prev + pv[:, HEAD_DIM:], alpha * acc_prev + pv[:, :HEAD_DIM]))
    for r in heads:
        m_ref[r], l_ref[r], acc_ref[r] = new[r]


def _flash_result(r, l_ref, acc_ref):
    return acc_ref[r] / jnp.maximum(l_ref[r], 1e-30)


def _stack_heads(q_ref, g, rep):
    return jnp.concatenate([q_ref[:, (g * rep + r) * HEAD_DIM:(g * rep + r + 1) * HEAD_DIM] for r in range(rep)],
                           axis=0)


def _nsa_prompt_kernel(qa_ref, qar_ref, kc_ref, vc_ref, ks_ref, vs_ref, kw_ref, vw_ref, gt_ref, ovt_ref, o_ref,
                       m_ref, l_ref, acc_ref, *, tq, n_sel):
    qi = pl.program_id(2)
    R = NSA_REP
    C = kc_ref.shape[0]
    NS = ovt_ref.shape[0]
    tk = tq
    q0 = qi * tq
    qpos = q0 + lax.broadcasted_iota(jnp.int32, (tq, 1), 0)
    qpos_l = q0 + lax.broadcasted_iota(jnp.int32, (1, tq), 1)

    kc = kc_ref[...]
    vc = vc_ref[...].astype(BF16)
    cmask = (lax.broadcasted_iota(jnp.int32, (C, 1), 0) * CMP_STRIDE + (CMP_LEN - 1)) <= qpos_l
    imp = jnp.zeros((NS, tq), F32)
    o_c = []
    for r in range(R):
        q = qa_ref[:, r * HEAD_DIM:(r + 1) * HEAD_DIM]
        s = lax.dot_general(kc, q, NT_DIMS, precision=HI, preferred_element_type=F32) * SCALE
        s = jnp.where(cmask, s, NEG)
        p = jnp.where(cmask, jnp.exp(s - jnp.max(s, axis=0, keepdims=True)), 0.0)
        p = p / jnp.maximum(jnp.sum(p, axis=0, keepdims=True), 1e-30)
        o_c.append(lax.dot_general(p.astype(BF16), vc, TN_DIMS, preferred_element_type=F32))
        imp = imp + jnp.dot(ovt_ref[...], p, precision=HI, preferred_element_type=F32)

    jb = lax.broadcasted_iota(jnp.int32, (NS, 1), 0)
    cur = qpos_l // SLC_BLK
    forced = (jb == 0) | (jb == cur) | (jb == cur - 1)
    score = jnp.where(forced, jnp.inf, jnp.where(jb <= cur, imp, -jnp.inf))
    sel = (_rank_desc(score, NS, 0) < n_sel).astype(F32).T.astype(BF16)

    qs = [(qar_ref[:, r * HEAD_DIM:(r + 1) * HEAD_DIM] * SCALE).astype(BF16) for r in range(R)]

    _flash_reset(m_ref, l_ref, acc_ref)

    ts = 2 * tq

    def sel_body(kt, _):
        k0 = pl.multiple_of(kt * ts, ts)
        k = ks_ref[pl.ds(k0, ts), :].astype(BF16)
        v = vs_ref[pl.ds(k0, ts), :].astype(BF16)
        kpos_e = k0 + lax.broadcasted_iota(jnp.int32, (NS, ts), 1)
        expand = (lax.broadcasted_iota(jnp.int32, (NS, ts), 0) == kpos_e // SLC_BLK).astype(BF16)
        selx = jnp.dot(sel, expand, preferred_element_type=F32)
        kpos = k0 + lax.broadcasted_iota(jnp.int32, (1, ts), 1)
        bias = jnp.where((selx > 0.5) & (kpos <= qpos), 0.0, NEG)
        _flash_heads(qs, k, v, [bias] * R, m_ref, l_ref, acc_ref)
        return 0

    lax.fori_loop(0, (q0 + tq + ts - 1) // ts, sel_body, 0)
    o_s = [_flash_result(r, l_ref, acc_ref) for r in range(R)]

    _flash_reset(m_ref, l_ref, acc_ref)
    k0 = pl.multiple_of(jnp.maximum(q0 - WINDOW, 0), tk)
    kpos = k0 + lax.broadcasted_iota(jnp.int32, (1, WINDOW), 1)
    bias = jnp.where((kpos < q0) & (qpos - kpos < WINDOW), 0.0, NEG)
    _flash_heads(qs, kw_ref[pl.ds(k0, WINDOW), :].astype(BF16), vw_ref[pl.ds(k0, WINDOW), :].astype(BF16),
                 [bias] * R, m_ref, l_ref, acc_ref)
    k0 = pl.multiple_of(q0, tk)
    causal = jnp.where(lax.broadcasted_iota(jnp.int32, (tq, tk), 1) <= lax.broadcasted_iota(jnp.int32, (tq, tk), 0),
                       0.0, NEG)
    _flash_heads(qs, kw_ref[pl.ds(k0, tk), :].astype(BF16), vw_ref[pl.ds(k0, tk), :].astype(BF16),
                 [causal] * R, m_ref, l_ref, acc_ref)

    gate = jax.nn.sigmoid(gt_ref[...])
    for r in range(R):
        o = (gate[:, 3 * r:3 * r + 1] * o_c[r] + gate[:, 3 * r + 1:3 * r + 2] * o_s[r]
             + gate[:, 3 * r + 2:3 * r + 3] * _flash_result(r, l_ref, acc_ref))
        o_ref[:, r * HEAD_DIM:(r + 1) * HEAD_DIM] = o.astype(o_ref.dtype)


def _overlap_matrix(C, NS):
    ci, sj = np.arange(C)[:, None], np.arange(NS)[None, :]
    return ((ci * CMP_STRIDE < (sj + 1) * SLC_BLK) & (ci * CMP_STRIDE + CMP_LEN > sj * SLC_BLK)).astype(np.float32)


def _nsa_prompt(zz3, kcvc):
    B, T, _ = zz3.shape
    tq = 256
    C = kcvc.shape[3]
    NS = -(-T // SLC_BLK)
    assert NS <= LANES and T % (2 * tq) == 0
    ovt = jnp.asarray(_overlap_matrix(C, NS).T)
    gw = NSA_REP * HEAD_DIM
    col = lambda c: c // HEAD_DIM
    kv_spec = lambda c: pl.BlockSpec((None, T, HEAD_DIM), lambda b, g, i: (b, 0, col(c) + g))
    return pl.pallas_call(
        functools.partial(_nsa_prompt_kernel, tq=tq, n_sel=min(SLC_TOPN, NS)),
        grid=(B, NSA_KV, T // tq),
        in_specs=[pl.BlockSpec((None, tq, gw), lambda b, g, i: (b, i, C_QA // gw + g)),
                  pl.BlockSpec((None, tq, gw), lambda b, g, i: (b, i, C_QAR // gw + g)),
                  pl.BlockSpec((None, None, None, C, HEAD_DIM), lambda b, g, i: (0, b, g, 0, 0)),
                  pl.BlockSpec((None, None, None, C, HEAD_DIM), lambda b, g, i: (1, b, g, 0, 0)),
                  kv_spec(C_NSA + 2 * ZT), kv_spec(C_NSA + 3 * ZT), kv_spec(C_WIN), kv_spec(C_WIN + ZT),
                  pl.BlockSpec((None, tq, HEAD_DIM), lambda b, g, i: (b, i, col(C_MISC) + g)),
                  pl.BlockSpec((NS, C), lambda b, g, i: (0, 0))],
        out_specs=pl.BlockSpec((None, tq, gw), lambda b, g, i: (b, i, g)),
        out_shape=jax.ShapeDtypeStruct((B, T, NSA_HEADS * HEAD_DIM), BF16),
        scratch_shapes=[pltpu.VMEM((NSA_REP, tq, LANES), F32), pltpu.VMEM((NSA_REP, tq, LANES), F32),
                        pltpu.VMEM((NSA_REP, tq, HEAD_DIM), F32)],
        compiler_params=_cparams(("parallel", "parallel", "arbitrary")),
        name="nsa_prompt",
    )(zz3, zz3, kcvc, kcvc, zz3, zz3, zz3, zz3, zz3, ovt)


def _moba_prompt_kernel(q_ref, kb_ref, vb_ref, o_ref, kmean_ref, m_ref, l_ref, acc_ref, *, tq, k_top):
    qi = pl.program_id(2)
    R = MOBA_REP
    NB = kmean_ref.shape[0]
    tk = MOBA_BLK
    own = qi

    @pl.when(qi == 0)
    def _():
        kmean_ref[...] = jnp.sum(kb_ref[...].reshape(NB, MOBA_BLK, HEAD_DIM), axis=1) * (1.0 / MOBA_BLK)

    nbi = lax.broadcasted_iota(jnp.int32, (1, NB), 1)
    past = nbi < own
    hidden = []
    for r in range(R):
        q = q_ref[:, r * HEAD_DIM:(r + 1) * HEAD_DIM]
        sg = lax.dot_general(q, kmean_ref[...], NT_DIMS, precision=HI, preferred_element_type=F32)
        sg = jnp.where(past, sg, -jnp.inf)
        hid = jnp.where((_rank_desc(sg, NB, 1) < k_top) & past, 0.0, NEG)
        hidden.append(jnp.concatenate([hid, jnp.zeros((tq, LANES - NB), F32)], axis=1).astype(BF16))
    qm = [jnp.concatenate([(q_ref[:, r * HEAD_DIM:(r + 1) * HEAD_DIM] * SCALE).astype(BF16), hidden[r]], axis=1)
          for r in range(R)]

    _flash_reset(m_ref, l_ref, acc_ref)

    def tile(k0, size, masked, bias):
        v = vb_ref[pl.ds(k0, size), :].astype(BF16)
        kblk = (k0 + lax.broadcasted_iota(jnp.int32, (size, LANES), 0)) // MOBA_BLK
        onehot = jnp.where((lax.broadcasted_iota(jnp.int32, (size, LANES), 1) == kblk) & masked, 1.0, 0.0)
        km = jnp.concatenate([kb_ref[pl.ds(k0, size), :].astype(BF16), onehot.astype(BF16)], axis=1)
        _flash_heads(qm, km, v, [bias] * R, m_ref, l_ref, acc_ref)

    def body(n, _):
        tile(pl.multiple_of(n * 2 * tk, 2 * tk), 2 * tk, True, None)
        return 0

    lax.fori_loop(0, (qi + 1) // 2, body, 0)
    causal = jnp.where(lax.broadcasted_iota(jnp.int32, (tq, tk), 1) <= lax.broadcasted_iota(jnp.int32, (tq, tk), 0),
                       0.0, NEG)
    tile(pl.multiple_of(qi * tk, tk), tk, False, causal)
    for r in range(R):
        o_ref[:, r * HEAD_DIM:(r + 1) * HEAD_DIM] = _flash_result(r, l_ref, acc_ref).astype(o_ref.dtype)


def _moba_prompt(zz3):
    B, T, _ = zz3.shape
    tq = MOBA_BLK
    NB = T // MOBA_BLK
    assert NB <= LANES and NB % 2 == 0
    gw = MOBA_REP * HEAD_DIM
    col = lambda c: c // HEAD_DIM
    return pl.pallas_call(
        functools.partial(_moba_prompt_kernel, tq=tq, k_top=min(MOBA_TOPK, NB)),
        grid=(B, MOBA_KV, T // tq),
        in_specs=[pl.BlockSpec((None, tq, gw), lambda b, g, i: (b, i, C_QB // gw + g)),
                  pl.BlockSpec((None, T, HEAD_DIM), lambda b, g, i: (b, 0, col(C_MOBA) + g)),
                  pl.BlockSpec((None, T, HEAD_DIM), lambda b, g, i: (b, 0, col(C_MOBA + ZT) + g))],
        out_specs=pl.BlockSpec((None, tq, gw), lambda b, g, i: (b, i, g)),
        out_shape=jax.ShapeDtypeStruct((B, T, MOBA_HEADS * HEAD_DIM), BF16),
        scratch_shapes=[pltpu.VMEM((NB, HEAD_DIM), F32),
                        pltpu.VMEM((MOBA_REP, tq, LANES), F32), pltpu.VMEM((MOBA_REP, tq, LANES), F32),
                        pltpu.VMEM((MOBA_REP, tq, HEAD_DIM), F32)],
        compiler_params=_cparams(("parallel", "parallel", "arbitrary")),
        name="moba_prompt",
    )(zz3, zz3, zz3)


def _gla_kernel(q_ref, k_ref, v_ref, r_ref, misc_ref, gw2_ref, gb_ref, gng_ref, s0_ref, og_ref, s_ref,
                lg_ref, st_ref, *, chunk):
    cg = pl.program_id(1)
    tt = q_ref.shape[0]

    @pl.when(cg == 0)
    def _():
        for h in range(GLA_HEADS):
            st_ref[h] = s0_ref[h].T

    ga = misc_ref[:, 16:16 + GLA_RANK]
    pre = jnp.dot(ga, gw2_ref[...], precision=HI, preferred_element_type=F32) + gb_ref[...]
    lg_ref[...] = jax.nn.log_sigmoid(pre) * (1.0 / GLA_TAU)

    ri = lax.broadcasted_iota(jnp.int32, (chunk, chunk), 0)
    ci = lax.broadcasted_iota(jnp.int32, (chunk, chunk), 1)
    causal = ci <= ri
    tri = causal.astype(F32)
    mid = chunk // 2

    def body(c, _):
        r0 = pl.multiple_of(c * chunk, chunk)
        rows = pl.ds(r0, chunk)
        b_all = jnp.dot(tri, lg_ref[rows, :], precision=HI, preferred_element_type=F32)
        q_all = q_ref[rows, :] * (GLA_DK ** -0.5)
        k_all = k_ref[rows, :]
        v_all = v_ref[rows, :]
        r_all = r_ref[rows, :]
        st_all = [st_ref[h] for h in range(GLA_HEADS)]
        st_new, o_new = [], []
        for h in range(GLA_HEADS):
            ks = slice(h * GLA_DK, (h + 1) * GLA_DK)
            vs = slice(h * GLA_DV, (h + 1) * GLA_DV)
            b = b_all[:, ks]
            bm = b[mid:mid + 1, :]
            bl = b[chunk - 1:chunk, :]
            q = q_all[:, ks]
            k = k_all[:, ks]
            vb = v_all[:, vs].astype(BF16)
            a = lax.dot_general((q * jnp.exp(b - bm)).astype(BF16), (k * jnp.exp(bm - b)).astype(BF16), NT_DIMS,
                                preferred_element_type=F32)
            a = jnp.where(causal, a, 0.0)
            st = st_all[h]
            o = lax.dot_general((q * jnp.exp(b)).astype(BF16), st.astype(BF16), NT_DIMS,
                                preferred_element_type=F32)
            o = o + jnp.dot(a.astype(BF16), vb, preferred_element_type=F32)
            kd = (k * jnp.exp(bl - b)).astype(BF16)
            st_new.append(st * jnp.exp(bl) + lax.dot_general(vb, kd, TN_DIMS, preferred_element_type=F32))
            gr = r_all[:, vs]
            o_new.append((_head_rms(o, gng_ref[...]) * (gr * jax.nn.sigmoid(gr))).astype(og_ref.dtype))
        for h in range(GLA_HEADS):
            st_ref[h] = st_new[h]
        og_ref[rows, :] = jnp.concatenate(o_new, axis=1)
        return 0

    lax.fori_loop(0, tt // chunk, body, 0)

    @pl.when(cg == pl.num_programs(1) - 1)
    def _():
        for h in range(GLA_HEADS):
            s_ref[h] = st_ref[h].T


def _gla(zz3, gw2, gb, gng, l, s0, tt, chunk):
    B, T, _ = zz3.shape
    nq = GLA_HEADS * GLA_DK
    nv = GLA_HEADS * GLA_DV
    return pl.pallas_call(
        functools.partial(_gla_kernel, chunk=chunk),
        grid=(B, T // tt),
        in_specs=[pl.BlockSpec((None, tt, nq), lambda b, c: (b, c, C_GQ // nq)),
                  pl.BlockSpec((None, tt, nq), lambda b, c: (b, c, C_GK // nq)),
                  pl.BlockSpec((None, tt, nv), lambda b, c: (b, c, C_GV // nv)),
                  pl.BlockSpec((None, tt, nv), lambda b, c: (b, c, C_GR // nv)),
                  pl.BlockSpec((None, tt, HEAD_DIM), lambda b, c: (b, c, C_MISC // HEAD_DIM)),
                  pl.BlockSpec((None, GLA_RANK, nq), lambda b, c: (l, 0, 0)),
                  pl.BlockSpec((None, 1, nq), lambda b, c: (l, 0, 0)),
                  pl.BlockSpec((None, 1, GLA_DV), lambda b, c: (l, 0, 0)),
                  pl.BlockSpec((None, GLA_HEADS, GLA_DK, GLA_DV), lambda b, c: (b, 0, 0, 0))],
        out_specs=[pl.BlockSpec((None, tt, nv), lambda b, c: (b, c, 0)),
                   pl.BlockSpec((None, GLA_HEADS, GLA_DK, GLA_DV), lambda b, c: (b, 0, 0, 0))],
        out_shape=[jax.ShapeDtypeStruct((B, T, nv), BF16),
                   jax.ShapeDtypeStruct((B, GLA_HEADS, GLA_DK, GLA_DV), F32)],
        scratch_shapes=[pltpu.VMEM((tt, nq), F32), pltpu.VMEM((GLA_HEADS, GLA_DV, GLA_DK), F32)],
        compiler_params=_cparams(("parallel", "arbitrary")),
        name="gla",
    )(zz3, zz3, zz3, zz3, zz3, gw2, gb, gng, s0)


def _page_copies(cache_ref, pt_ref, l, buf_ref, sem_ref, j0, b, pg, slot):
    page = cache_ref.shape[2]
    return [pltpu.make_async_copy(cache_ref.at[l, pt_ref[b, pg * PAGES_PER_STEP + k], :, j0 + cb, :],
                                  buf_ref.at[slot, cb, pl.ds(k * page, page), :],
                                  sem_ref.at[slot]) for k in range(PAGES_PER_STEP) for cb in range(buf_ref.shape[1])]


def _paged_fetch(cache_ref, pt_ref, l, buf_ref, sem_ref, j0):
    b, pg = pl.program_id(0), pl.program_id(1)
    npg = pl.num_programs(1)
    i = b * npg + pg
    slot = i % 2
    copies = functools.partial(_page_copies, cache_ref, pt_ref, l, buf_ref, sem_ref, j0)

    @pl.when(i == 0)
    def _():
        for c in copies(b, pg, slot):
            c.start()

    @pl.when(i + 1 < pl.num_programs(0) * npg)
    def _():
        n = i + 1
        for c in copies(n // npg, n % npg, 1 - slot):
            c.start()

    for c in copies(b, pg, slot):
        c.wait()
    return slot


def _decode_qpos(past_len, td, rep):
    t = lax.broadcasted_iota(jnp.int32, (rep * td, 1), 0) % td
    return past_len + t


def _nsa_dec_select_kernel(pt_ref, qa_ref, cache_ref, wp_ref, pe_ref, w1_ref, b1_ref, w2_ref, gk_ref, ov_ref,
                           oc_ref, sel_ref, buf_ref, sem_ref, pab_ref, *, l, past_len, n_blocks, n_sel):
    pg = pl.program_id(1)
    slot = _paged_fetch(cache_ref, pt_ref, l, buf_ref, sem_ref, 0)
    n = buf_ref.shape[2] // CMP_STRIDE
    C = pab_ref.shape[2]
    td = qa_ref.shape[0]
    R = NSA_REP
    NSP = ov_ref.shape[1]

    for kv in range(2):
        acc = jnp.zeros((NSA_KV * n, 2 * CMP_HID), F32)
        for sp in range(CMP_STRIDE // 2):
            parts = []
            for g in range(NSA_KV):
                cb = kv * NSA_KV + g
                xa = buf_ref[slot, cb, pl.ds(2 * sp, n, stride=CMP_STRIDE), :]
                xb = buf_ref[slot, cb, pl.ds(2 * sp + 1, n, stride=CMP_STRIDE), :]
                parts.append(jnp.concatenate([xa, xb], axis=1))
            x = jnp.concatenate(parts, axis=0).astype(BF16)
            acc += jnp.dot(x, wp_ref[kv, sp], preferred_element_type=F32)
        for g in range(NSA_KV):
            pab_ref[kv, g, pl.ds(pl.multiple_of(pg * n, n), n), :] = acc[g * n:(g + 1) * n]

    @pl.when(pg == pl.num_programs(1) - 1)
    def _():
        qpos = _decode_qpos(past_len, td, 1)
        qpos3 = _decode_qpos(past_len, td, R)
        cmask = (lax.broadcasted_iota(jnp.int32, (1, C), 1) * CMP_STRIDE + (CMP_LEN - 1)) <= qpos3
        toks = []
        for kv in range(2):
            pe8 = jnp.broadcast_to(pe_ref[kv], (8, CMP_LEN * HEAD_DIM))
            const = b1_ref[kv] + jnp.dot(pe8, w1_ref[kv], precision=HI, preferred_element_type=F32)[0:1]
            per_g = []
            for g in range(NSA_KV):
                pab = pab_ref[kv, g]
                hid = const + pab[:, :CMP_HID] + pltpu.roll(pab[:, CMP_HID:], C - 1, 0)
                tok = jnp.dot(jax.nn.gelu(hid), w2_ref[kv], precision=HI, preferred_element_type=F32)
                per_g.append(_head_rms(tok, gk_ref[...]) if kv == 0 else tok)
            toks.append(per_g)
        jb = lax.broadcasted_iota(jnp.int32, (1, NSP), 1)
        cur = qpos // SLC_BLK
        forced = (jb == 0) | (jb == cur) | (jb == cur - 1)
        for g in range(NSA_KV):
            kc, vc = toks[0][g], toks[1][g]
            q3 = _stack_heads(qa_ref, g, R)
            s = lax.dot_general(q3, kc, NT_DIMS, precision=HI, preferred_element_type=F32) * SCALE
            s = jnp.where(cmask, s, NEG)
            p = jnp.where(cmask, jnp.exp(s - jnp.max(s, axis=-1, keepdims=True)), 0.0)
            p = p / jnp.maximum(jnp.sum(p, axis=-1, keepdims=True), 1e-30)
            oc = jnp.dot(p.astype(BF16), vc.astype(BF16), preferred_element_type=F32)
            imp = jnp.zeros((td, NSP), F32)
            for r in range(R):
                h = g * R + r
                oc_ref[:, h * HEAD_DIM:(h + 1) * HEAD_DIM] = oc[r * td:(r + 1) * td]
                imp = imp + jnp.dot(p[r * td:(r + 1) * td], ov_ref[...], precision=HI, preferred_element_type=F32)
            score = jnp.where(forced, jnp.inf, jnp.where(jb <= cur, imp, -jnp.inf))
            sel = (_rank_desc(score, n_blocks, 1) < n_sel).astype(F32)
            sel_ref[g] = jnp.concatenate([sel] * R, axis=0)


def _nsa_dec_select(page_table, zs3, cache4, wp, pe, w1, b1, w2, gk, l, past_len):
    DB, TD, _ = zs3.shape
    n_pages = page_table.shape[1]
    page = cache4.shape[2]
    npg = n_pages // PAGES_PER_STEP
    C = past_len // CMP_STRIDE
    NS = -(-(past_len + TD) // SLC_BLK)
    NSP = -(-NS // LANES) * LANES
    ov = jnp.asarray(_overlap_matrix(C, NSP))
    qw = NSA_HEADS * HEAD_DIM
    rows = PAGES_PER_STEP * page
    grid_spec = pltpu.PrefetchScalarGridSpec(
        num_scalar_prefetch=1,
        grid=(DB, npg),
        in_specs=[pl.BlockSpec((None, TD, qw), lambda b, p, pt: (b, 0, C_QA // qw)),
                  pl.BlockSpec(memory_space=pl.ANY),
                  pl.BlockSpec((None, 2, CMP_STRIDE // 2, 2 * HEAD_DIM, 2 * CMP_HID), lambda b, p, pt: (l, 0, 0, 0, 0)),
                  pl.BlockSpec((None, 2, 1, CMP_LEN * HEAD_DIM), lambda b, p, pt: (l, 0, 0, 0)),
                  pl.BlockSpec((None, 2, CMP_LEN * HEAD_DIM, CMP_HID), lambda b, p, pt: (l, 0, 0, 0)),
                  pl.BlockSpec((None, 2, 1, CMP_HID), lambda b, p, pt: (l, 0, 0, 0)),
                  pl.BlockSpec((None, 2, CMP_HID, HEAD_DIM), lambda b, p, pt: (l, 0, 0, 0)),
                  pl.BlockSpec((None, None, 1, HEAD_DIM), lambda b, p, pt: (l, 1, 0, 0)),
                  pl.BlockSpec((C, NSP), lambda b, p, pt: (0, 0))],
        out_specs=[pl.BlockSpec((None, TD, qw), lambda b, p, pt: (b, 0, 0)),
                   pl.BlockSpec((None, NSA_KV, NSA_REP * TD, NSP), lambda b, p, pt: (b, 0, 0, 0))],
        scratch_shapes=[pltpu.VMEM((2, 2 * NSA_KV, rows, HEAD_DIM), F32), pltpu.SemaphoreType.DMA((2,)),
                        pltpu.VMEM((2, NSA_KV, C, 2 * CMP_HID), F32)])
    return pl.pallas_call(
        functools.partial(_nsa_dec_select_kernel, l=l, past_len=past_len, n_blocks=NS, n_sel=min(SLC_TOPN, NS)),
        grid_spec=grid_spec,
        out_shape=[jax.ShapeDtypeStruct((DB, TD, qw), F32),
                   jax.ShapeDtypeStruct((DB, NSA_KV, NSA_REP * TD, NSP), F32)],
        compiler_params=_cparams(("arbitrary", "arbitrary")),
        name="nsa_dec_select",
    )(page_table, zs3, cache4, wp, pe, w1, b1, w2, gk, ov)


def _paged_attn_kernel(pt_ref, q_ref, sel_ref, kn_ref, vn_ref, cache_ref, o_ref, buf_ref, sem_ref,
                       m_ref, l_ref, acc_ref, *, l, j0, blk, rep):
    pg = pl.program_id(1)
    slot = _paged_fetch(cache_ref, pt_ref, l, buf_ref, sem_ref, j0)
    tk = buf_ref.shape[2]
    td = q_ref.shape[0]
    NBP = sel_ref.shape[2]
    G = sel_ref.shape[0]

    @pl.when(pg == 0)
    def _():
        _flash_reset(m_ref, l_ref, acc_ref)

    kpos = pg * tk + lax.broadcasted_iota(jnp.int32, (NBP, tk), 1)
    expand = (lax.broadcasted_iota(jnp.int32, (NBP, tk), 0) == kpos // blk).astype(BF16)
    qs = [(_stack_heads(q_ref, g, rep) * SCALE).astype(BF16) for g in range(G)]
    for g in range(G):
        k = buf_ref[slot, g].astype(BF16)
        v = buf_ref[slot, G + g].astype(BF16)
        s = lax.dot_general(qs[g], k, NT_DIMS, preferred_element_type=F32)
        selx = jnp.dot(sel_ref[g].astype(BF16), expand, preferred_element_type=F32)
        m_ref[g], l_ref[g], acc_ref[g] = _online_update((m_ref[g], l_ref[g], acc_ref[g]), s, selx > 0.5, v)

    @pl.when(pg == pl.num_programs(1) - 1)
    def _():
        t = lax.broadcasted_iota(jnp.int32, (rep * td, 1), 0) % td
        causal = lax.broadcasted_iota(jnp.int32, (1, td), 1) <= t
        for g in range(G):
            k = kn_ref[:, g * HEAD_DIM:(g + 1) * HEAD_DIM].astype(BF16)
            v = vn_ref[:, g * HEAD_DIM:(g + 1) * HEAD_DIM].astype(BF16)
            s = lax.dot_general(qs[g], k, NT_DIMS, preferred_element_type=F32)
            o = _online_finish(_online_update((m_ref[g], l_ref[g], acc_ref[g]), s, causal, v))
            for r in range(rep):
                h = g * rep + r
                o_ref[:, h * HEAD_DIM:(h + 1) * HEAD_DIM] = o[r * td:(r + 1) * td]


def _paged_attn(page_table, zs3, sel, cache4, l, c_q, c_k, c_v, j0, blk, name):
    DB, TD, _ = zs3.shape
    G, rows_q, NBP = sel.shape[1:]
    rep = rows_q // TD
    n_pages = page_table.shape[1]
    page = cache4.shape[2]
    npg = n_pages // PAGES_PER_STEP
    qw = G * rep * HEAD_DIM
    kw = G * HEAD_DIM
    rows = PAGES_PER_STEP * page
    grid_spec = pltpu.PrefetchScalarGridSpec(
        num_scalar_prefetch=1,
        grid=(DB, npg),
        in_specs=[pl.BlockSpec((None, TD, qw), lambda b, p, pt: (b, 0, c_q // qw)),
                  pl.BlockSpec((None, G, rows_q, NBP), lambda b, p, pt: (b, 0, 0, 0)),
                  pl.BlockSpec((None, TD, kw), lambda b, p, pt: (b, 0, c_k // kw)),
                  pl.BlockSpec((None, TD, kw), lambda b, p, pt: (b, 0, c_v // kw)),
                  pl.BlockSpec(memory_space=pl.ANY)],
        out_specs=pl.BlockSpec((None, TD, qw), lambda b, p, pt: (b, 0, 0)),
        scratch_shapes=[pltpu.VMEM((2, 2 * G, rows, HEAD_DIM), F32), pltpu.SemaphoreType.DMA((2,)),
                        pltpu.VMEM((G, rows_q, 1), F32), pltpu.VMEM((G, rows_q, 1), F32),
                        pltpu.VMEM((G, rows_q, HEAD_DIM), F32)])
    return pl.pallas_call(
        functools.partial(_paged_attn_kernel, l=l, j0=j0, blk=blk, rep=rep),
        grid_spec=grid_spec,
        out_shape=jax.ShapeDtypeStruct((DB, TD, qw), F32),
        compiler_params=_cparams(("arbitrary", "arbitrary")),
        name=name,
    )(page_table, zs3, sel, zs3, zs3, cache4)


def _nsa_dec_combine_kernel(q_ref, oc_ref, os_ref, win_ref, wn_ref, gt_ref, o_ref):
    td = q_ref.shape[0]
    wb = win_ref.shape[0]
    R = NSA_REP
    t = lax.broadcasted_iota(jnp.int32, (R * td, 1), 0) % td
    d_buf = wb + t - lax.broadcasted_iota(jnp.int32, (1, wb), 1)
    d_new = t - lax.broadcasted_iota(jnp.int32, (1, td), 1)
    for g in range(NSA_KV):
        ksl = slice(g * HEAD_DIM, (g + 1) * HEAD_DIM)
        vsl = slice((NSA_KV + g) * HEAD_DIM, (NSA_KV + g + 1) * HEAD_DIM)
        q3 = (_stack_heads(q_ref, g, R) * SCALE).astype(BF16)
        carry = _online_init(R * td)
        s = lax.dot_general(q3, win_ref[:, ksl].astype(BF16), NT_DIMS, preferred_element_type=F32)
        carry = _online_update(carry, s, (d_buf >= 0) & (d_buf < WINDOW), win_ref[:, vsl].astype(BF16))
        s = lax.dot_general(q3, wn_ref[:, ksl].astype(BF16), NT_DIMS, preferred_element_type=F32)
        carry = _online_update(carry, s, (d_new >= 0) & (d_new < WINDOW), wn_ref[:, vsl].astype(BF16))
        o_w = _online_finish(carry)
        gate = jax.nn.sigmoid(gt_ref[:, g * LANES:(g + 1) * LANES])
        for r in range(R):
            cols = slice((g * R + r) * HEAD_DIM, (g * R + r + 1) * HEAD_DIM)
            o = (gate[:, 3 * r:3 * r + 1] * oc_ref[:, cols] + gate[:, 3 * r + 1:3 * r + 2] * os_ref[:, cols]
                 + gate[:, 3 * r + 2:3 * r + 3] * o_w[r * td:(r + 1) * td])
            o_ref[:, cols] = o.astype(o_ref.dtype)


def _nsa_dec_combine(zs3, oc, o_s, win4, l):
    DB, TD, _ = zs3.shape
    qw = NSA_HEADS * HEAD_DIM
    wb, ww = win4.shape[2], win4.shape[3]
    return pl.pallas_call(
        _nsa_dec_combine_kernel,
        grid=(DB,),
        in_specs=[pl.BlockSpec((None, TD, qw), lambda b: (b, 0, C_QAR // qw)),
                  pl.BlockSpec((None, TD, qw), lambda b: (b, 0, 0)),
                  pl.BlockSpec((None, TD, qw), lambda b: (b, 0, 0)),
                  pl.BlockSpec((None, None, wb, ww), lambda b: (l, b, 0, 0)),
                  pl.BlockSpec((None, TD, ww), lambda b: (b, 0, C_WIN // ww)),
                  pl.BlockSpec((None, TD, ZT), lambda b: (b, 0, C_MISC // ZT))],
        out_specs=pl.BlockSpec((None, TD, qw), lambda b: (b, 0, 0)),
        out_shape=jax.ShapeDtypeStruct((DB, TD, qw), BF16),
        compiler_params=_cparams(("parallel",)),
        name="nsa_dec_combine",
    )(zs3, oc, o_s, win4, zs3, zs3)


def _moba_dec_select_kernel(pt_ref, q_ref, cache_ref, sel_ref, buf_ref, sem_ref, kmean_ref,
                            *, l, past_len, k_top):
    pg = pl.program_id(1)
    slot = _paged_fetch(cache_ref, pt_ref, l, buf_ref, sem_ref, 0)
    bps = buf_ref.shape[2] // MOBA_BLK
    td = q_ref.shape[0]
    NBP = kmean_ref.shape[1]
    R = MOBA_REP

    @pl.when(pg == 0)
    def _():
        kmean_ref[...] = jnp.zeros(kmean_ref.shape, F32)

    for g in range(MOBA_KV):
        x = buf_ref[slot, g]
        kmean_ref[g, pl.ds(pl.multiple_of(pg * bps, bps), bps), :] = (
            jnp.sum(x.reshape(bps, MOBA_BLK, HEAD_DIM), axis=1) * (1.0 / MOBA_BLK))

    @pl.when(pg == pl.num_programs(1) - 1)
    def _():
        own = _decode_qpos(past_len, td, R) // MOBA_BLK
        past = lax.broadcasted_iota(jnp.int32, (1, NBP), 1) < own
        for g in range(MOBA_KV):
            q3 = _stack_heads(q_ref, g, R)
            sg = lax.dot_general(q3, kmean_ref[g], NT_DIMS, precision=HI, preferred_element_type=F32)
            sg = jnp.where(past, sg, -jnp.inf)
            sel_ref[g] = ((_rank_desc(sg, past_len // MOBA_BLK, 1) < k_top) & past).astype(F32)


def _moba_dec_select(page_table, zs3, cache4, l, past_len):
    DB, TD, _ = zs3.shape
    n_pages = page_table.shape[1]
    page = cache4.shape[2]
    npg = n_pages // PAGES_PER_STEP
    NB = -(-(past_len + TD) // MOBA_BLK)
    NBP = -(-NB // LANES) * LANES
    qw = MOBA_HEADS * HEAD_DIM
    rows = PAGES_PER_STEP * page
    grid_spec = pltpu.PrefetchScalarGridSpec(
        num_scalar_prefetch=1,
        grid=(DB, npg),
        in_specs=[pl.BlockSpec((None, TD, qw), lambda b, p, pt: (b, 0, C_QB // qw)),
                  pl.BlockSpec(memory_space=pl.ANY)],
        out_specs=pl.BlockSpec((None, MOBA_KV, MOBA_REP * TD, NBP), lambda b, p, pt: (b, 0, 0, 0)),
        scratch_shapes=[pltpu.VMEM((2, MOBA_KV, rows, HEAD_DIM), F32), pltpu.SemaphoreType.DMA((2,)),
                        pltpu.VMEM((MOBA_KV, NBP, HEAD_DIM), F32)])
    return pl.pallas_call(
        functools.partial(_moba_dec_select_kernel, l=l, past_len=past_len, k_top=min(MOBA_TOPK, NB)),
        grid_spec=grid_spec,
        out_shape=jax.ShapeDtypeStruct((DB, MOBA_KV, MOBA_REP * TD, NBP), F32),
        compiler_params=_cparams(("arbitrary", "arbitrary")),
        name="moba_dec_select",
    )(page_table, zs3, cache4)


def _permute_w_in(w_in):
    sizes = (768, 256, 256, 256, 256, 256, 256, 18, 768, 256, 256, 256, 256, 512, 16, 512)
    offs = np.concatenate([[0], np.cumsum(sizes)])
    (nq, nkc, nvc, nks, nvs, nkw, nvw, ngt, mq, mk, mv, gq, gk, gv, ga, gr) = [
        w_in[:, :, int(offs[i]):int(offs[i + 1])] for i in range(len(sizes))]
    L, D = w_in.shape[0], w_in.shape[1]
    z = lambda n: jnp.zeros((L, D, n), w_in.dtype)
    misc = jnp.concatenate([ngt[:, :, :9], z(7), ga, z(96), ngt[:, :, 9:], z(119)], axis=-1)
    wz = jnp.concatenate([nq, nq, nkc, nvc, nks, nvs, nkw, nvw, mq, mk, mv, gq, gv, gr, gk, misc], axis=-1)
    assert wz.shape[-1] == NZ
    return wz.astype(BF16)


def _pair_cmp_w1(w1):
    L = w1.shape[0]
    half = CMP_STRIDE // 2
    w = w1.reshape(L, 2, CMP_LEN // CMP_STRIDE, half, 2, HEAD_DIM, CMP_HID)
    w = w.transpose(0, 1, 3, 4, 5, 2, 6)
    return w.reshape(L, 2, half, 2 * HEAD_DIM, (CMP_LEN // CMP_STRIDE) * CMP_HID).astype(BF16)


def _gain_vector(nsa_g, moba_g):
    L = nsa_g.shape[0]
    gz = jnp.ones((L, NZ), F32)
    put = lambda gz, c, g, n: gz.at[:, c:c + n * HEAD_DIM].set(jnp.tile(g, (1, n)))
    gz = put(gz, C_QA, nsa_g[:, 0], NSA_HEADS)
    gz = put(gz, C_QAR, nsa_g[:, 0], NSA_HEADS)
    gz = put(gz, C_NSA + 2 * ZT, nsa_g[:, 2], NSA_KV)
    gz = put(gz, C_WIN, nsa_g[:, 3], NSA_KV)
    gz = put(gz, C_QB, moba_g[:, 0], MOBA_HEADS)
    gz = put(gz, C_MOBA, moba_g[:, 1], MOBA_KV)
    return gz.reshape(L, 1, NZ)


def _rope_tables(pos):
    half = HEAD_DIM // 2
    inv = ROPE_THETA ** (-jnp.arange(half, dtype=F32) / half)
    ang = pos.astype(F32)[:, None] * inv
    cos, sin = jnp.cos(ang), jnp.sin(ang)
    return jnp.concatenate([cos, cos], axis=-1), jnp.concatenate([-sin, sin], axis=-1)


def kernel(x_prompt, x_sample, cache_nsa, cache_moba, state_nsa_win, state_gla, page_table, c_prompt, c_sample,
           w_ada, b_ada, norm1_g, norm2_g, w_in, nsa_qk_g, nsa_cmp_pos, nsa_cmp_w1, nsa_cmp_b1, nsa_cmp_w2,
           moba_qk_g, gla_w_a2, gla_b_a, gla_norm_g, w_out, w_up, w_down):
    B, T, D = x_prompt.shape
    DB, TD, _ = x_sample.shape
    L = w_in.shape[0]
    n_pool, page = cache_nsa.shape[1], cache_nsa.shape[2]
    past_len = page_table.shape[1] * page
    assert page_table.shape[1] % PAGES_PER_STEP == 0 and (PAGES_PER_STEP * page) % MOBA_BLK == 0
    assert T % MOBA_BLK == 0 and WINDOW % MOBA_BLK == 0
    pos_p = jnp.arange(T, dtype=jnp.int32)
    pos_s = past_len + jnp.arange(TD, dtype=jnp.int32)
    MS = DB * TD

    wz = _permute_w_in(w_in)
    gz = _gain_vector(nsa_qk_g, moba_qk_g)
    wo = w_out.astype(BF16)
    wu = w_up.astype(BF16)
    wd = w_down.astype(BF16)
    wp = _pair_cmp_w1(nsa_cmp_w1)
    n1 = norm1_g.reshape(L, 1, D)
    n2 = norm2_g.reshape(L, 1, D)
    b1 = nsa_cmp_b1.reshape(L, 2, 1, CMP_HID)
    pe_flat = nsa_cmp_pos.reshape(L, 2, 1, CMP_LEN * HEAD_DIM)
    nsa_g4 = nsa_qk_g.reshape(L, 4, 1, HEAD_DIM)
    gb = gla_b_a.reshape(L, 1, GLA_HEADS * GLA_DK)
    gng = gla_norm_g.reshape(L, 1, GLA_DV)
    cos_p, sin_p = _rope_tables(pos_p)
    cos_s, sin_s = _rope_tables(jnp.tile(pos_s, DB))
    cache_nsa4 = cache_nsa.reshape(L, n_pool, page, 4 * NSA_KV, HEAD_DIM)
    cache_moba4 = cache_moba.reshape(L, n_pool, page, 2 * MOBA_KV, HEAD_DIM)
    win4 = state_nsa_win.reshape(L, DB, state_nsa_win.shape[2], 2 * NSA_KV * HEAD_DIM)

    nc = B + DB
    c_all = jnp.concatenate([c_prompt, c_sample, jnp.zeros((-nc % 8, D), F32)], axis=0)
    mod = _ada(c_all, w_ada, b_ada)

    def mods(l):
        mp = mod[l, :B].reshape(B, 1, 6, D)
        ms = jnp.repeat(mod[l, B:nc], TD, axis=0).reshape(1, MS, 6, D)
        return [mp[:, :, i] for i in range(6)], [ms[:, :, i] for i in range(6)]

    tm_p = 1024
    tpb_p = T // tm_p
    xp = x_prompt.reshape(B * T, D)
    xs = x_sample.reshape(MS, D)
    s0_p = jnp.zeros((B, GLA_HEADS, GLA_DK, GLA_DV), F32)
    outs = [[] for _ in range(8)]
    for l in range(L):
        (sh1, sc1, gt1, sh2, sc2, gt2), (sh1s, sc1s, gt1s, sh2s, sc2s, gt2s) = mods(l)
        zz = _nmm_zz(xp, sc1, sh1, n1, wz, l, gz, cos_p, sin_p, tm_p, tpb_p)
        zz3 = zz.reshape(B, T, NZ)
        kcvc = _compress_prompt(zz3, nsa_cmp_pos, nsa_cmp_w1, b1, nsa_cmp_w2, nsa_g4, l)
        oa = _nsa_prompt(zz3, kcvc)
        ob = _moba_prompt(zz3)
        og, s_fin = _gla(zz3, gla_w_a2, gb, gng, l, s0_p, 512, GLA_CHUNK)
        xp = _outproj(oa.reshape(B * T, -1), ob.reshape(B * T, -1), og.reshape(B * T, -1), wo, l, xp, gt1,
                      tm_p, tpb_p)
        u = _nmm_relu2(xp, sc2, sh2, n2, wu, l, tm_p, tpb_p)
        xp = _down(u, wd, l, xp, gt2, tm_p, tpb_p)
        outs[0].append(zz3[:, :, C_NSA:C_NSA + 4 * ZT].reshape(B, T, 4, NSA_KV, HEAD_DIM))
        outs[2].append(zz3[:, T - min(WINDOW, T):, C_WIN:C_WIN + 2 * ZT].reshape(B, -1, 2, NSA_KV, HEAD_DIM))
        outs[4].append(zz3[:, :, C_MOBA:C_MOBA + 2 * ZT].reshape(B, T, 2, MOBA_KV, HEAD_DIM))
        outs[6].append(s_fin)
        zs = _nmm_zz(xs, sc1s, sh1s, n1, wz, l, gz, cos_s, sin_s, MS, 1)
        zs3 = zs.reshape(DB, TD, NZ)
        oc_s, sel_a = _nsa_dec_select(page_table, zs3, cache_nsa4, wp, pe_flat, nsa_cmp_w1, b1, nsa_cmp_w2,
                                      nsa_g4, l, past_len)
        os_s = _paged_attn(page_table, zs3, sel_a, cache_nsa4, l, C_QAR, C_NSA + 2 * ZT, C_NSA + 3 * ZT,
                           2 * NSA_KV, SLC_BLK, "nsa_dec_selected")
        oa_s = _nsa_dec_combine(zs3, oc_s, os_s, win4, l)
        sel_b = _moba_dec_select(page_table, zs3, cache_moba4, l, past_len)
        ob_s = _paged_attn(page_table, zs3, sel_b, cache_moba4, l, C_QB, C_MOBA, C_MOBA + ZT, 0, MOBA_BLK,
                           "moba_dec_attn").astype(BF16)
        og_s, s_fin_s = _gla(zs3, gla_w_a2, gb, gng, l, state_gla[l], TD, TD)
        xs = _outproj(oa_s.reshape(MS, -1), ob_s.reshape(MS, -1), og_s.reshape(MS, -1), wo, l, xs, gt1s, MS, 1)
        us = _nmm_relu2(xs, sc2s, sh2s, n2, wu, l, MS, 1)
        xs = _down(us, wd, l, xs, gt2s, MS, 1)
        win_rows_s = zs3[:, :, C_WIN:C_WIN + 2 * ZT].reshape(DB, TD, 2, NSA_KV, HEAD_DIM)
        win_ctx = jnp.concatenate([state_nsa_win[l], win_rows_s], axis=1)
        outs[1].append(zs3[:, :, C_NSA:C_NSA + 4 * ZT].reshape(DB, TD, 4, NSA_KV, HEAD_DIM))
        outs[3].append(win_ctx[:, -min(WINDOW, past_len + TD):])
        outs[5].append(zs3[:, :, C_MOBA:C_MOBA + 2 * ZT].reshape(DB, TD, 2, MOBA_KV, HEAD_DIM))
        outs[7].append(s_fin_s)
    st = [jnp.stack(o) for o in outs]
    return (xp.reshape(B, T, D), xs.reshape(DB, TD, D), st[0], st[1], st[2], st[3], st[4], st[5], st[6], st[7])
```

```python
import functools

import numpy as np
import jax
import jax.numpy as jnp
from jax import lax
from jax.experimental import pallas as pl
from jax.experimental.pallas import tpu as pltpu

F32 = jnp.float32
BF16 = jnp.bfloat16
HI = lax.Precision.HIGHEST

D_MODEL = 2048
HEAD_DIM = 128
NSA_HEADS = 6
NSA_KV = 2
NSA_REP = NSA_HEADS // NSA_KV
MOBA_HEADS = 6
MOBA_KV = 2
MOBA_REP = MOBA_HEADS // MOBA_KV
GLA_HEADS = 4
GLA_DK = 64
GLA_DV = 128
GLA_RANK = 16
GLA_TAU = 16.0
GLA_CHUNK = 64
CMP_LEN = 32
CMP_STRIDE = 16
CMP_HID = 128
SLC_BLK = 64
SLC_TOPN = 16
WINDOW = 512
MOBA_BLK = 256
MOBA_TOPK = 3
D_FF = 4 * D_MODEL
ROPE_THETA = 10000.0
EPS = 1e-6
SCALE = HEAD_DIM ** -0.5
NEG = -1e30
LANES = 128

ZT = 256
C_QA = 0
C_QAR = 768
C_NSA = 1536
C_WIN = 2560
C_QB = 3072
C_MOBA = 3840
C_GQ = 4352
C_GV = 4608
C_GR = 5120
C_GK = 5632
C_MISC = 5888
NZ = 6144

VMEM_LIMIT = 56 * 1024 * 1024
PAGES_PER_STEP = 16

NT_DIMS = (((1,), (1,)), ((), ()))
TN_DIMS = (((0,), (0,)), ((), ()))


def _cparams(sem):
    return pltpu.CompilerParams(dimension_semantics=sem, vmem_limit_bytes=VMEM_LIMIT)


def _ada_kernel(c_ref, w_ref, b_ref, o_ref):
    c = c_ref[...]
    a = c * jax.nn.sigmoid(c)
    o_ref[...] = jnp.dot(a, w_ref[...], precision=HI, preferred_element_type=F32) + b_ref[...]


def _ada(c_all, w_ada, b_ada):
    L, D, N = w_ada.shape
    R = c_all.shape[0]
    tn = 512
    return pl.pallas_call(
        _ada_kernel,
        grid=(L, N // tn),
        in_specs=[pl.BlockSpec((R, D), lambda l, j: (0, 0)),
                  pl.BlockSpec((None, D, tn), lambda l, j: (l, 0, j)),
                  pl.BlockSpec((None, 1, tn), lambda l, j: (l, 0, j))],
        out_specs=pl.BlockSpec((None, R, tn), lambda l, j: (l, 0, j)),
        out_shape=jax.ShapeDtypeStruct((L, R, N), F32),
        compiler_params=_cparams(("parallel", "parallel")),
        name="ada_mod",
    )(c_all, w_ada, b_ada.reshape(L, 1, N))


def _mod_norm(x_ref, sc_ref, sh_ref, gn_ref, h_ref):
    x = x_ref[...]
    y = x * lax.rsqrt(jnp.mean(x * x, axis=-1, keepdims=True) + EPS) * gn_ref[...]
    h_ref[...] = (y * (1.0 + sc_ref[...]) + sh_ref[...]).astype(BF16)


def _head_rms(a, g):
    return a * lax.rsqrt(jnp.mean(a * a, axis=-1, keepdims=True) + EPS) * g


def _nmm_zz_kernel(x_ref, sc_ref, sh_ref, gn_ref, w_ref, gz_ref, cos_ref, sin_ref, o_ref, h_ref):
    j = pl.program_id(1)

    @pl.when(j == 0)
    def _():
        _mod_norm(x_ref, sc_ref, sh_ref, gn_ref, h_ref)

    acc = jnp.dot(h_ref[...], w_ref[...], preferred_element_type=F32)
    for t in range(o_ref.shape[1] // ZT):
        jt = j * (o_ref.shape[1] // ZT) + t
        is_norm = jt < 3
        is_rope = ((jt >= 3) & (jt <= 5)) | (jt == 8) | (jt == 10) | ((jt >= 12) & (jt <= 15))
        heads = [slice(t * ZT + hh * HEAD_DIM, t * ZT + (hh + 1) * HEAD_DIM) for hh in range(ZT // HEAD_DIM)]

        @pl.when(jnp.logical_not(is_norm | is_rope))
        def _():
            o_ref[:, t * ZT:(t + 1) * ZT] = acc[:, t * ZT:(t + 1) * ZT]

        @pl.when(is_norm)
        def _():
            for sl in heads:
                o_ref[:, sl] = _head_rms(acc[:, sl], gz_ref[:, sl])

        @pl.when(is_rope)
        def _():
            for sl in heads:
                y = _head_rms(acc[:, sl], gz_ref[:, sl])
                o_ref[:, sl] = y * cos_ref[...] + pltpu.roll(y, HEAD_DIM // 2, 1) * sin_ref[...]


def _nmm_relu2_kernel(x_ref, sc_ref, sh_ref, gn_ref, w_ref, o_ref, h_ref):
    @pl.when(pl.program_id(1) == 0)
    def _():
        _mod_norm(x_ref, sc_ref, sh_ref, gn_ref, h_ref)

    acc = jnp.dot(h_ref[...], w_ref[...], preferred_element_type=F32)
    r = jnp.maximum(acc, 0.0)
    o_ref[...] = (r * r).astype(o_ref.dtype)


def _mod_specs(sc, tpb):
    rows = sc.shape[1]
    D = sc.shape[2]
    return pl.BlockSpec((None, rows, D), lambda i, j: (i // tpb, 0, 0))


def _nmm_zz(x, sc, sh, gn, wz, l, gz, cos, sin, tm, tpb):
    M, D = x.shape
    tn = 2 * ZT
    return pl.pallas_call(
        _nmm_zz_kernel,
        grid=(M // tm, NZ // tn),
        in_specs=[pl.BlockSpec((tm, D), lambda i, j: (i, 0)),
                  _mod_specs(sc, tpb), _mod_specs(sh, tpb),
                  pl.BlockSpec((None, 1, D), lambda i, j: (l, 0, 0)),
                  pl.BlockSpec((None, D, tn), lambda i, j: (l, 0, j)),
                  pl.BlockSpec((None, 1, tn), lambda i, j: (l, 0, j)),
                  pl.BlockSpec((tm, HEAD_DIM), lambda i, j: (i % tpb, 0)),
                  pl.BlockSpec((tm, HEAD_DIM), lambda i, j: (i % tpb, 0))],
        out_specs=pl.BlockSpec((tm, tn), lambda i, j: (i, j)),
        out_shape=jax.ShapeDtypeStruct((M, NZ), F32),
        scratch_shapes=[pltpu.VMEM((tm, D), BF16)],
        compiler_params=_cparams(("parallel", "arbitrary")),
        name="in_proj",
    )(x, sc, sh, gn, wz, gz, cos, sin)


def _nmm_relu2(x, sc, sh, gn, wu, l, tm, tpb):
    M, D = x.shape
    N = wu.shape[2]
    tn = 512
    return pl.pallas_call(
        _nmm_relu2_kernel,
        grid=(M // tm, N // tn),
        in_specs=[pl.BlockSpec((tm, D), lambda i, j: (i, 0)),
                  _mod_specs(sc, tpb), _mod_specs(sh, tpb),
                  pl.BlockSpec((None, 1, D), lambda i, j: (l, 0, 0)),
                  pl.BlockSpec((None, D, tn), lambda i, j: (l, 0, j))],
        out_specs=pl.BlockSpec((tm, tn), lambda i, j: (i, j)),
        out_shape=jax.ShapeDtypeStruct((M, N), BF16),
        scratch_shapes=[pltpu.VMEM((tm, D), BF16)],
        compiler_params=_cparams(("parallel", "arbitrary")),
        name="mlp_up",
    )(x, sc, sh, gn, wu)


def _outproj_kernel(oa_ref, ob_ref, og_ref, w_ref, x_ref, gt_ref, o_ref):
    na, nb = oa_ref.shape[1], ob_ref.shape[1]
    acc = jnp.dot(oa_ref[...], w_ref[0:na, :], preferred_element_type=F32)
    acc += jnp.dot(ob_ref[...], w_ref[na:na + nb, :], preferred_element_type=F32)
    acc += jnp.dot(og_ref[...], w_ref[na + nb:, :], preferred_element_type=F32)
    o_ref[...] = x_ref[...] + gt_ref[...] * acc


def _outproj(oa, ob, og, wo, l, x, gt, tm, tpb):
    M, D = x.shape
    tn = 512
    rows = gt.shape[1]
    return pl.pallas_call(
        _outproj_kernel,
        grid=(M // tm, D // tn),
        in_specs=[pl.BlockSpec((tm, oa.shape[1]), lambda i, j: (i, 0)),
                  pl.BlockSpec((tm, ob.shape[1]), lambda i, j: (i, 0)),
                  pl.BlockSpec((tm, og.shape[1]), lambda i, j: (i, 0)),
                  pl.BlockSpec((None, D, tn), lambda i, j: (l, 0, j)),
                  pl.BlockSpec((tm, tn), lambda i, j: (i, j)),
                  pl.BlockSpec((None, rows, tn), lambda i, j: (i // tpb, 0, j))],
        out_specs=pl.BlockSpec((tm, tn), lambda i, j: (i, j)),
        out_shape=jax.ShapeDtypeStruct((M, D), F32),
        compiler_params=_cparams(("parallel", "parallel")),
        name="out_proj",
    )(oa, ob, og, wo, x, gt)


def _down_kernel(u_ref, w_ref, x_ref, gt_ref, o_ref, acc_ref):
    k = pl.program_id(2)

    @pl.when(k == 0)
    def _():
        acc_ref[...] = jnp.zeros_like(acc_ref)

    acc_ref[...] += jnp.dot(u_ref[...], w_ref[...], preferred_element_type=F32)

    @pl.when(k == pl.num_programs(2) - 1)
    def _():
        o_ref[...] = x_ref[...] + gt_ref[...] * acc_ref[...]


def _down(u, wd, l, x, gt, tm, tpb):
    M, D = x.shape
    K = u.shape[1]
    tn, tk = 512, 2048
    rows = gt.shape[1]
    return pl.pallas_call(
        _down_kernel,
        grid=(M // tm, D // tn, K // tk),
        in_specs=[pl.BlockSpec((tm, tk), lambda i, j, k: (i, k)),
                  pl.BlockSpec((None, tk, tn), lambda i, j, k: (l, k, j)),
                  pl.BlockSpec((tm, tn), lambda i, j, k: (i, j)),
                  pl.BlockSpec((None, rows, tn), lambda i, j, k: (i // tpb, 0, j))],
        out_specs=pl.BlockSpec((tm, tn), lambda i, j, k: (i, j)),
        out_shape=jax.ShapeDtypeStruct((M, D), F32),
        scratch_shapes=[pltpu.VMEM((tm, tn), F32)],
        compiler_params=_cparams(("parallel", "parallel", "arbitrary")),
        name="mlp_down",
    )(u, wd, x, gt)


def _mlp_kernel(x_ref, sc_ref, sh_ref, gn_ref, wu_ref, wd_ref, gt_ref, o_ref, h_ref, acc_ref):
    f = pl.program_id(1)

    @pl.when(f == 0)
    def _():
        _mod_norm(x_ref, sc_ref, sh_ref, gn_ref, h_ref)
        acc_ref[...] = jnp.zeros_like(acc_ref)

    r = jnp.maximum(jnp.dot(h_ref[...], wu_ref[...], preferred_element_type=F32), 0.0)
    acc_ref[...] += jnp.dot((r * r).astype(BF16), wd_ref[...], preferred_element_type=F32)

    @pl.when(f == pl.num_programs(1) - 1)
    def _():
        o_ref[...] = x_ref[...] + gt_ref[...] * acc_ref[...]


def _mlp(x, sc, sh, gn, wu, wd, l, gt, tm, tpb):
    M, D = x.shape
    F = wu.shape[2]
    tf = 512
    rows = gt.shape[1]
    return pl.pallas_call(
        _mlp_kernel,
        grid=(M // tm, F // tf),
        in_specs=[pl.BlockSpec((tm, D), lambda i, f: (i, 0)),
                  _mod_specs(sc, tpb), _mod_specs(sh, tpb),
                  pl.BlockSpec((None, 1, D), lambda i, f: (l, 0, 0)),
                  pl.BlockSpec((None, D, tf), lambda i, f: (l, 0, f)),
                  pl.BlockSpec((None, tf, D), lambda i, f: (l, f, 0)),
                  pl.BlockSpec((None, rows, D), lambda i, f: (i // tpb, 0, 0))],
        out_specs=pl.BlockSpec((tm, D), lambda i, f: (i, 0)),
        out_shape=jax.ShapeDtypeStruct((M, D), F32),
        scratch_shapes=[pltpu.VMEM((tm, D), BF16), pltpu.VMEM((tm, D), F32)],
        compiler_params=_cparams(("parallel", "arbitrary")),
        name="mlp",
    )(x, sc, sh, gn, wu, wd, gt)


def _compress_kernel(rows_ref, pe_ref, w1_ref, b1_ref, w2_ref, g_ref, o_ref, *, nch):
    slot = pl.program_id(1)
    pa = jnp.zeros((nch, CMP_HID), F32)
    pb = jnp.zeros((nch, CMP_HID), F32)
    for s in range(CMP_STRIDE):
        xs = rows_ref[pl.ds(s, nch, stride=CMP_STRIDE), :]
        wa = w1_ref[s * HEAD_DIM:(s + 1) * HEAD_DIM, :]
        wb = w1_ref[(CMP_STRIDE + s) * HEAD_DIM:(CMP_STRIDE + s + 1) * HEAD_DIM, :]
        pa += jnp.dot(xs + pe_ref[s:s + 1, :], wa, precision=HI, preferred_element_type=F32)
        pb += jnp.dot(xs + pe_ref[CMP_STRIDE + s:CMP_STRIDE + s + 1, :], wb, precision=HI,
                      preferred_element_type=F32)
    hid = b1_ref[...] + pa + pltpu.roll(pb, nch - 1, 0)
    tok = jnp.dot(jax.nn.gelu(hid), w2_ref[...], precision=HI, preferred_element_type=F32)

    @pl.when(slot == 0)
    def _():
        o_ref[...] = _head_rms(tok, g_ref[...])

    @pl.when(slot != 0)
    def _():
        o_ref[...] = tok


def _compress_prompt(zz3, pe, w1, b1, w2, gk, l):
    B, T, _ = zz3.shape
    nch = T // CMP_STRIDE
    cb = C_NSA // HEAD_DIM
    return pl.pallas_call(
        functools.partial(_compress_kernel, nch=nch),
        grid=(B, 2, NSA_KV),
        in_specs=[pl.BlockSpec((None, T, HEAD_DIM), lambda b, s, g: (b, 0, cb + 2 * s + g)),
                  pl.BlockSpec((None, None, CMP_LEN, HEAD_DIM), lambda b, s, g: (l, s, 0, 0)),
                  pl.BlockSpec((None, None, CMP_LEN * HEAD_DIM, CMP_HID), lambda b, s, g: (l, s, 0, 0)),
                  pl.BlockSpec((None, None, 1, CMP_HID), lambda b, s, g: (l, s, 0, 0)),
                  pl.BlockSpec((None, None, CMP_HID, HEAD_DIM), lambda b, s, g: (l, s, 0, 0)),
                  pl.BlockSpec((None, None, 1, HEAD_DIM), lambda b, s, g: (l, 1, 0, 0))],
        out_specs=pl.BlockSpec((None, None, None, nch, HEAD_DIM), lambda b, s, g: (s, b, g, 0, 0)),
        out_shape=jax.ShapeDtypeStruct((2, B, NSA_KV, nch, HEAD_DIM), F32),
        compiler_params=_cparams(("parallel", "parallel", "parallel")),
        name="nsa_compress",
    )(zz3, pe, w1, b1, w2, gk)


def _rank_desc(score, n, axis):
    idx = lax.broadcasted_iota(jnp.int32, score.shape, axis)
    rank = jnp.zeros(score.shape, jnp.int32)
    for i in range(n):
        ci = score[:, i:i + 1] if axis == 1 else score[i:i + 1, :]
        beats = (ci > score) | ((ci == score) & (idx > i))
        rank = rank + beats.astype(jnp.int32)
    return rank


def _online_update(carry, s, msk, v):
    m, l, acc = carry
    s = jnp.where(msk, s, NEG)
    m_new = jnp.maximum(m, jnp.max(s, axis=-1, keepdims=True))
    alpha = jnp.exp(m - m_new)
    p = jnp.where(msk, jnp.exp(s - m_new), 0.0)
    l = alpha * l + jnp.sum(p, axis=-1, keepdims=True)
    acc = alpha * acc + jnp.dot(p.astype(BF16), v, preferred_element_type=F32)
    return m_new, l, acc


def _online_init(rows):
    return (jnp.full((rows, 1), NEG, F32), jnp.zeros((rows, 1), F32), jnp.zeros((rows, HEAD_DIM), F32))


def _online_finish(carry):
    _, l, acc = carry
    return acc / jnp.maximum(l, 1e-30)


def _flash_reset(m_ref, l_ref, acc_ref):
    m_ref[...] = jnp.full(m_ref.shape, NEG, F32)
    l_ref[...] = jnp.zeros(l_ref.shape, F32)
    acc_ref[...] = jnp.zeros(acc_ref.shape, F32)


def _flash_heads(qs, k, v, biases, m_ref, l_ref, acc_ref):
    heads = range(len(qs))
    state = [(m_ref[r], l_ref[r], acc_ref[r]) for r in heads]
    v1 = jnp.concatenate([v, jnp.ones_like(v)], axis=1)
    new = []
    for r in heads:
        m_prev, l_prev, acc_prev = state[r]
        s = lax.dot_general(qs[r], k, NT_DIMS, preferred_element_type=F32)
        if biases[r] is not None:
            s = s + biases[r]
        m_new = jnp.maximum(m_prev, jnp.max(s, axis=-1, keepdims=True))
        alpha = jnp.exp(m_prev - m_new)
        p = jnp.exp(s - jnp.concatenate([m_new] * (s.shape[1] // LANES), axis=1))
        pv = jnp.dot(p.astype(BF16), v1, preferred_element_type=F32)
        new.append((m_new, alpha * l_prev + pv[:, HEAD_DIM:], alpha * acc_prev + pv[:, :HEAD_DIM]))
    for r in heads:
        m_ref[r], l_ref[r], acc_ref[r] = new[r]


def _flash_result(r, l_ref, acc_ref):
    return acc_ref[r] / jnp.maximum(l_ref[r], 1e-30)


def _stack_heads(q_ref, g, rep):
    return jnp.concatenate([q_ref[:, (g * rep + r) * HEAD_DIM:(g * rep + r + 1) * HEAD_DIM] for r in range(rep)],
                           axis=0)


def _nsa_prompt_kernel(qa_ref, qar_ref, kc_ref, vc_ref, ks_ref, vs_ref, kw_ref, vw_ref, gt_ref, ovt_ref, o_ref,
                       m_ref, l_ref, acc_ref, *, tq, n_sel):
    qi = pl.program_id(2)
    R = NSA_REP
    C = kc_ref.shape[0]
    NS = ovt_ref.shape[0]
    tk = tq
    q0 = qi * tq
    qpos = q0 + lax.broadcasted_iota(jnp.int32, (tq, 1), 0)
    qpos_l = q0 + lax.broadcasted_iota(jnp.int32, (1, tq), 1)

    kc = kc_ref[...]
    vc = vc_ref[...].astype(BF16)
    cmask = (lax.broadcasted_iota(jnp.int32, (C, 1), 0) * CMP_STRIDE + (CMP_LEN - 1)) <= qpos_l
    imp = jnp.zeros((NS, tq), F32)
    o_c = []
    for r in range(R):
        q = qa_ref[:, r * HEAD_DIM:(r + 1) * HEAD_DIM]
        s = lax.dot_general(kc, q, NT_DIMS, precision=HI, preferred_element_type=F32) * SCALE
        s = jnp.where(cmask, s, NEG)
        p = jnp.where(cmask, jnp.exp(s - jnp.max(s, axis=0, keepdims=True)), 0.0)
        p = p / jnp.maximum(jnp.sum(p, axis=0, keepdims=True), 1e-30)
        o_c.append(lax.dot_general(p.astype(BF16), vc, TN_DIMS, preferred_element_type=F32))
        imp = imp + jnp.dot(ovt_ref[...], p, precision=HI, preferred_element_type=F32)

    jb = lax.broadcasted_iota(jnp.int32, (NS, 1), 0)
    cur = qpos_l // SLC_BLK
    forced = (jb == 0) | (jb == cur) | (jb == cur - 1)
    score = jnp.where(forced, jnp.inf, jnp.where(jb <= cur, imp, -jnp.inf))
    sel = (_rank_desc(score, NS, 0) < n_sel).astype(F32).T.astype(BF16)

    qs = [(qar_ref[:, r * HEAD_DIM:(r + 1) * HEAD_DIM] * SCALE).astype(BF16) for r in range(R)]

    _flash_reset(m_ref, l_ref, acc_ref)

    ts = 2 * tq

    def sel_body(kt, _):
        k0 = pl.multiple_of(kt * ts, ts)
        k = ks_ref[pl.ds(k0, ts), :].astype(BF16)
        v = vs_ref[pl.ds(k0, ts), :].astype(BF16)
        kpos_e = k0 + lax.broadcasted_iota(jnp.int32, (NS, ts), 1)
        expand = (lax.broadcasted_iota(jnp.int32, (NS, ts), 0) == kpos_e // SLC_BLK).astype(BF16)
        selx = jnp.dot(sel, expand, preferred_element_type=F32)
        kpos = k0 + lax.broadcasted_iota(jnp.int32, (1, ts), 1)
        bias = jnp.where((selx > 0.5) & (kpos <= qpos), 0.0, NEG)
        _flash_heads(qs, k, v, [bias] * R, m_ref, l_ref, acc_ref)
        return 0

    lax.fori_loop(0, (q0 + tq + ts - 1) // ts, sel_body, 0)
    o_s = [_flash_result(r, l_ref, acc_ref) for r in range(R)]

    _flash_reset(m_ref, l_ref, acc_ref)
    k0 = pl.multiple_of(jnp.maximum(q0 - WINDOW, 0), tk)
    kpos = k0 + lax.broadcasted_iota(jnp.int32, (1, WINDOW), 1)
    bias = jnp.where((kpos < q0) & (qpos - kpos < WINDOW), 0.0, NEG)
    _flash_heads(qs, kw_ref[pl.ds(k0, WINDOW), :].astype(BF16), vw_ref[pl.ds(k0, WINDOW), :].astype(BF16),
                 [bias] * R, m_ref, l_ref, acc_ref)
    k0 = pl.multiple_of(q0, tk)
    causal = jnp.where(lax.broadcasted_iota(jnp.int32, (tq, tk), 1) <= lax.broadcasted_iota(jnp.int32, (tq, tk), 0),
                       0.0, NEG)
    _flash_heads(qs, kw_ref[pl.ds(k0, tk), :].astype(BF16), vw_ref[pl.ds(k0, tk), :].astype(BF16),
                 [causal] * R, m_ref, l_ref, acc_ref)

    gate = jax.nn.sigmoid(gt_ref[...])
    for r in range(R):
        o = (gate[:, 3 * r:3 * r + 1] * o_c[r] + gate[:, 3 * r + 1:3 * r + 2] * o_s[r]
             + gate[:, 3 * r + 2:3 * r + 3] * _flash_result(r, l_ref, acc_ref))
        o_ref[:, r * HEAD_DIM:(r + 1) * HEAD_DIM] = o.astype(o_ref.dtype)


def _overlap_matrix(C, NS):
    ci, sj = np.arange(C)[:, None], np.arange(NS)[None, :]
    return ((ci * CMP_STRIDE < (sj + 1) * SLC_BLK) & (ci * CMP_STRIDE + CMP_LEN > sj * SLC_BLK)).astype(np.float32)


def _nsa_prompt(zz3, kcvc):
    B, T, _ = zz3.shape
    tq = 256
    C = kcvc.shape[3]
    NS = -(-T // SLC_BLK)
    assert NS <= LANES and T % (2 * tq) == 0
    ovt = jnp.asarray(_overlap_matrix(C, NS).T)
    gw = NSA_REP * HEAD_DIM
    col = lambda c: c // HEAD_DIM
    kv_spec = lambda c: pl.BlockSpec((None, T, HEAD_DIM), lambda b, g, i: (b, 0, col(c) + g))
    return pl.pallas_call(
        functools.partial(_nsa_prompt_kernel, tq=tq, n_sel=min(SLC_TOPN, NS)),
        grid=(B, NSA_KV, T // tq),
        in_specs=[pl.BlockSpec((None, tq, gw), lambda b, g, i: (b, i, C_QA // gw + g)),
                  pl.BlockSpec((None, tq, gw), lambda b, g, i: (b, i, C_QAR // gw + g)),
                  pl.BlockSpec((None, None, None, C, HEAD_DIM), lambda b, g, i: (0, b, g, 0, 0)),
                  pl.BlockSpec((None, None, None, C, HEAD_DIM), lambda b, g, i: (1, b, g, 0, 0)),
                  kv_spec(C_NSA + 2 * ZT), kv_spec(C_NSA + 3 * ZT), kv_spec(C_WIN), kv_spec(C_WIN + ZT),
                  pl.BlockSpec((None, tq, HEAD_DIM), lambda b, g, i: (b, i, col(C_MISC) + g)),
                  pl.BlockSpec((NS, C), lambda b, g, i: (0, 0))],
        out_specs=pl.BlockSpec((None, tq, gw), lambda b, g, i: (b, i, g)),
        out_shape=jax.ShapeDtypeStruct((B, T, NSA_HEADS * HEAD_DIM), BF16),
        scratch_shapes=[pltpu.VMEM((NSA_REP, tq, LANES), F32), pltpu.VMEM((NSA_REP, tq, LANES), F32),
                        pltpu.VMEM((NSA_REP, tq, HEAD_DIM), F32)],
        compiler_params=_cparams(("parallel", "parallel", "arbitrary")),
        name="nsa_prompt",
    )(zz3, zz3, kcvc, kcvc, zz3, zz3, zz3, zz3, zz3, ovt)


def _moba_prompt_kernel(q_ref, kb_ref, vb_ref, o_ref, kmean_ref, m_ref, l_ref, acc_ref, *, tq, k_top):
    qi = pl.program_id(2)
    R = MOBA_REP
    NB = kmean_ref.shape[0]
    tk = MOBA_BLK
    own = qi

    @pl.when(qi == 0)
    def _():
        kmean_ref[...] = jnp.sum(kb_ref[...].reshape(NB, MOBA_BLK, HEAD_DIM), axis=1) * (1.0 / MOBA_BLK)

    nbi = lax.broadcasted_iota(jnp.int32, (1, NB), 1)
    past = nbi < own
    hidden = []
    for r in range(R):
        q = q_ref[:, r * HEAD_DIM:(r + 1) * HEAD_DIM]
        sg = lax.dot_general(q, kmean_ref[...], NT_DIMS, precision=HI, preferred_element_type=F32)
        sg = jnp.where(past, sg, -jnp.inf)
        hid = jnp.where((_rank_desc(sg, NB, 1) < k_top) & past, 0.0, NEG)
        hidden.append(jnp.concatenate([hid, jnp.zeros((tq, LANES - NB), F32)], axis=1).astype(BF16))
    qm = [jnp.concatenate([(q_ref[:, r * HEAD_DIM:(r + 1) * HEAD_DIM] * SCALE).astype(BF16), hidden[r]], axis=1)
          for r in range(R)]

    _flash_reset(m_ref, l_ref, acc_ref)

    def tile(k0, size, masked, bias):
        v = vb_ref[pl.ds(k0, size), :].astype(BF16)
        kblk = (k0 + lax.broadcasted_iota(jnp.int32, (size, LANES), 0)) // MOBA_BLK
        onehot = jnp.where((lax.broadcasted_iota(jnp.int32, (size, LANES), 1) == kblk) & masked, 1.0, 0.0)
        km = jnp.concatenate([kb_ref[pl.ds(k0, size), :].astype(BF16), onehot.astype(BF16)], axis=1)
        _flash_heads(qm, km, v, [bias] * R, m_ref, l_ref, acc_ref)

    def body(n, _):
        tile(pl.multiple_of(n * 2 * tk, 2 * tk), 2 * tk, True, None)
        return 0

    lax.fori_loop(0, (qi + 1) // 2, body, 0)
    causal = jnp.where(lax.broadcasted_iota(jnp.int32, (tq, tk), 1) <= lax.broadcasted_iota(jnp.int32, (tq, tk), 0),
                       0.0, NEG)
    tile(pl.multiple_of(qi * tk, tk), tk, False, causal)
    for r in range(R):
        o_ref[:, r * HEAD_DIM:(r + 1) * HEAD_DIM] = _flash_result(r, l_ref, acc_ref).astype(o_ref.dtype)


def _moba_prompt(zz3):
    B, T, _ = zz3.shape
    tq = MOBA_BLK
    NB = T // MOBA_BLK
    assert NB <= LANES and NB % 2 == 0
    gw = MOBA_REP * HEAD_DIM
    col = lambda c: c // HEAD_DIM
    return pl.pallas_call(
        functools.partial(_moba_prompt_kernel, tq=tq, k_top=min(MOBA_TOPK, NB)),
        grid=(B, MOBA_KV, T // tq),
        in_specs=[pl.BlockSpec((None, tq, gw), lambda b, g, i: (b, i, C_QB // gw + g)),
                  pl.BlockSpec((None, T, HEAD_DIM), lambda b, g, i: (b, 0, col(C_MOBA) + g)),
                  pl.BlockSpec((None, T, HEAD_DIM), lambda b, g, i: (b, 0, col(C_MOBA + ZT) + g))],
        out_specs=pl.BlockSpec((None, tq, gw), lambda b, g, i: (b, i, g)),
        out_shape=jax.ShapeDtypeStruct((B, T, MOBA_HEADS * HEAD_DIM), BF16),
        scratch_shapes=[pltpu.VMEM((NB, HEAD_DIM), F32),
                        pltpu.VMEM((MOBA_REP, tq, LANES), F32), pltpu.VMEM((MOBA_REP, tq, LANES), F32),
                        pltpu.VMEM((MOBA_REP, tq, HEAD_DIM), F32)],
        compiler_params=_cparams(("parallel", "parallel", "arbitrary")),
        name="moba_prompt",
    )(zz3, zz3, zz3)


def _gla_kernel(q_ref, k_ref, v_ref, r_ref, misc_ref, gw2_ref, gb_ref, gng_ref, s0_ref, og_ref, s_ref,
                lg_ref, st_ref, *, chunk):
    cg = pl.program_id(1)
    tt = q_ref.shape[0]

    @pl.when(cg == 0)
    def _():
        for h in range(GLA_HEADS):
            st_ref[h] = s0_ref[h].T

    ga = misc_ref[:, 16:16 + GLA_RANK]
    pre = jnp.dot(ga, gw2_ref[...], precision=HI, preferred_element_type=F32) + gb_ref[...]
    lg_ref[...] = jax.nn.log_sigmoid(pre) * (1.0 / GLA_TAU)

    ri = lax.broadcasted_iota(jnp.int32, (chunk, chunk), 0)
    ci = lax.broadcasted_iota(jnp.int32, (chunk, chunk), 1)
    causal = ci <= ri
    tri = causal.astype(F32)
    mid = chunk // 2

    def body(c, _):
        r0 = pl.multiple_of(c * chunk, chunk)
        rows = pl.ds(r0, chunk)
        b_all = jnp.dot(tri, lg_ref[rows, :], precision=HI, preferred_element_type=F32)
        q_all = q_ref[rows, :] * (GLA_DK ** -0.5)
        k_all = k_ref[rows, :]
        v_all = v_ref[rows, :]
        r_all = r_ref[rows, :]
        st_all = [st_ref[h] for h in range(GLA_HEADS)]
        st_new, o_new = [], []
        for h in range(GLA_HEADS):
            ks = slice(h * GLA_DK, (h + 1) * GLA_DK)
            vs = slice(h * GLA_DV, (h + 1) * GLA_DV)
            b = b_all[:, ks]
            bm = b[mid:mid + 1, :]
            bl = b[chunk - 1:chunk, :]
            q = q_all[:, ks]
            k = k_all[:, ks]
            vb = v_all[:, vs].astype(BF16)
            a = lax.dot_general((q * jnp.exp(b - bm)).astype(BF16), (k * jnp.exp(bm - b)).astype(BF16), NT_DIMS,
                                preferred_element_type=F32)
            a = jnp.where(causal, a, 0.0)
            st = st_all[h]
            o = lax.dot_general((q * jnp.exp(b)).astype(BF16), st.astype(BF16), NT_DIMS,
                                preferred_element_type=F32)
            o = o + jnp.dot(a.astype(BF16), vb, preferred_element_type=F32)
            kd = (k * jnp.exp(bl - b)).astype(BF16)
            st_new.append(st * jnp.exp(bl) + lax.dot_general(vb, kd, TN_DIMS, preferred_element_type=F32))
            gr = r_all[:, vs]
            o_new.append((_head_rms(o, gng_ref[...]) * (gr * jax.nn.sigmoid(gr))).astype(og_ref.dtype))
        for h in range(GLA_HEADS):
            st_ref[h] = st_new[h]
        og_ref[rows, :] = jnp.concatenate(o_new, axis=1)
        return 0

    lax.fori_loop(0, tt // chunk, body, 0)

    @pl.when(cg == pl.num_programs(1) - 1)
    def _():
        for h in range(GLA_HEADS):
            s_ref[h] = st_ref[h].T


def _gla(zz3, gw2, gb, gng, l, s0, tt, chunk):
    B, T, _ = zz3.shape
    nq = GLA_HEADS * GLA_DK
    nv = GLA_HEADS * GLA_DV
    return pl.pallas_call(
        functools.partial(_gla_kernel, chunk=chunk),
        grid=(B, T // tt),
        in_specs=[pl.BlockSpec((None, tt, nq), lambda b, c: (b, c, C_GQ // nq)),
                  pl.BlockSpec((None, tt, nq), lambda b, c: (b, c, C_GK // nq)),
                  pl.BlockSpec((None, tt, nv), lambda b, c: (b, c, C_GV // nv)),
                  pl.BlockSpec((None, tt, nv), lambda b, c: (b, c, C_GR // nv)),
                  pl.BlockSpec((None, tt, HEAD_DIM), lambda b, c: (b, c, C_MISC // HEAD_DIM)),
                  pl.BlockSpec((None, GLA_RANK, nq), lambda b, c: (l, 0, 0)),
                  pl.BlockSpec((None, 1, nq), lambda b, c: (l, 0, 0)),
                  pl.BlockSpec((None, 1, GLA_DV), lambda b, c: (l, 0, 0)),
                  pl.BlockSpec((None, GLA_HEADS, GLA_DK, GLA_DV), lambda b, c: (b, 0, 0, 0))],
        out_specs=[pl.BlockSpec((None, tt, nv), lambda b, c: (b, c, 0)),
                   pl.BlockSpec((None, GLA_HEADS, GLA_DK, GLA_DV), lambda b, c: (b, 0, 0, 0))],
        out_shape=[jax.ShapeDtypeStruct((B, T, nv), BF16),
                   jax.ShapeDtypeStruct((B, GLA_HEADS, GLA_DK, GLA_DV), F32)],
        scratch_shapes=[pltpu.VMEM((tt, nq), F32), pltpu.VMEM((GLA_HEADS, GLA_DV, GLA_DK), F32)],
        compiler_params=_cparams(("parallel", "arbitrary")),
        name="gla",
    )(zz3, zz3, zz3, zz3, zz3, gw2, gb, gng, s0)


def _page_copies(cache_ref, pt_ref, l, buf_ref, sem_ref, j0, b, pg, slot):
    page = cache_ref.shape[2]
    return [pltpu.make_async_copy(cache_ref.at[l, pt_ref[b, pg * PAGES_PER_STEP + k], :, j0 + cb, :],
                                  buf_ref.at[slot, cb, pl.ds(k * page, page), :],
                                  sem_ref.at[slot]) for k in range(PAGES_PER_STEP) for cb in range(buf_ref.shape[1])]


def _paged_fetch(cache_ref, pt_ref, l, buf_ref, sem_ref, j0):
    b, pg = pl.program_id(0), pl.program_id(1)
    npg = pl.num_programs(1)
    i = b * npg + pg
    slot = i % 2
    copies = functools.partial(_page_copies, cache_ref, pt_ref, l, buf_ref, sem_ref, j0)

    @pl.when(i == 0)
    def _():
        for c in copies(b, pg, slot):
            c.start()

    @pl.when(i + 1 < pl.num_programs(0) * npg)
    def _():
        n = i + 1
        for c in copies(n // npg, n % npg, 1 - slot):
            c.start()

    for c in copies(b, pg, slot):
        c.wait()
    return slot


def _decode_qpos(past_len, td, rep):
    t = lax.broadcasted_iota(jnp.int32, (rep * td, 1), 0) % td
    return past_len + t


def _nsa_dec_select_kernel(pt_ref, qa_ref, cache_ref, wp_ref, pe_ref, w1_ref, b1_ref, w2_ref, gk_ref, ov_ref,
                           oc_ref, sel_ref, buf_ref, sem_ref, pab_ref, *, l, past_len, n_blocks, n_sel):
    pg = pl.program_id(1)
    slot = _paged_fetch(cache_ref, pt_ref, l, buf_ref, sem_ref, 0)
    n = buf_ref.shape[2] // CMP_STRIDE
    C = pab_ref.shape[2]
    td = qa_ref.shape[0]
    R = NSA_REP
    NSP = ov_ref.shape[1]

    for kv in range(2):
        acc = jnp.zeros((NSA_KV * n, 2 * CMP_HID), F32)
        for sp in range(CMP_STRIDE // 2):
            parts = []
            for g in range(NSA_KV):
                cb = kv * NSA_KV + g
                xa = buf_ref[slot, cb, pl.ds(2 * sp, n, stride=CMP_STRIDE), :]
                xb = buf_ref[slot, cb, pl.ds(2 * sp + 1, n, stride=CMP_STRIDE), :]
                parts.append(jnp.concatenate([xa, xb], axis=1))
            x = jnp.concatenate(parts, axis=0).astype(BF16)
            acc += jnp.dot(x, wp_ref[kv, sp], preferred_element_type=F32)
        for g in range(NSA_KV):
            pab_ref[kv, g, pl.ds(pl.multiple_of(pg * n, n), n), :] = acc[g * n:(g + 1) * n]

    @pl.when(pg == pl.num_programs(1) - 1)
    def _():
        qpos = _decode_qpos(past_len, td, 1)
        qpos3 = _decode_qpos(past_len, td, R)
        cmask = (lax.broadcasted_iota(jnp.int32, (1, C), 1) * CMP_STRIDE + (CMP_LEN - 1)) <= qpos3
        toks = []
        for kv in range(2):
            pe8 = jnp.broadcast_to(pe_ref[kv], (8, CMP_LEN * HEAD_DIM))
            const = b1_ref[kv] + jnp.dot(pe8, w1_ref[kv], precision=HI, preferred_element_type=F32)[0:1]
            per_g = []
            for g in range(NSA_KV):
                pab = pab_ref[kv, g]
                hid = const + pab[:, :CMP_HID] + pltpu.roll(pab[:, CMP_HID:], C - 1, 0)
                tok = jnp.dot(jax.nn.gelu(hid), w2_ref[kv], precision=HI, preferred_element_type=F32)
                per_g.append(_head_rms(tok, gk_ref[...]) if kv == 0 else tok)
            toks.append(per_g)
        jb = lax.broadcasted_iota(jnp.int32, (1, NSP), 1)
        cur = qpos // SLC_BLK
        forced = (jb == 0) | (jb == cur) | (jb == cur - 1)
        for g in range(NSA_KV):
            kc, vc = toks[0][g], toks[1][g]
            q3 = _stack_heads(qa_ref, g, R)
            s = lax.dot_general(q3, kc, NT_DIMS, precision=HI, preferred_element_type=F32) * SCALE
            s = jnp.where(cmask, s, NEG)
            p = jnp.where(cmask, jnp.exp(s - jnp.max(s, axis=-1, keepdims=True)), 0.0)
            p = p / jnp.maximum(jnp.sum(p, axis=-1, keepdims=True), 1e-30)
            oc = jnp.dot(p.astype(BF16), vc.astype(BF16), preferred_element_type=F32)
            imp = jnp.zeros((td, NSP), F32)
            for r in range(R):
                h = g * R + r
                oc_ref[:, h * HEAD_DIM:(h + 1) * HEAD_DIM] = oc[r * td:(r + 1) * td]
                imp = imp + jnp.dot(p[r * td:(r + 1) * td], ov_ref[...], precision=HI, preferred_element_type=F32)
            score = jnp.where(forced, jnp.inf, jnp.where(jb <= cur, imp, -jnp.inf))
            sel = (_rank_desc(score, n_blocks, 1) < n_sel).astype(F32)
            sel_ref[g] = jnp.concatenate([sel] * R, axis=0)


def _nsa_dec_select(page_table, zs3, cache4, wp, pe, w1, b1, w2, gk, l, past_len):
    DB, TD, _ = zs3.shape
    n_pages = page_table.shape[1]
    page = cache4.shape[2]
    npg = n_pages // PAGES_PER_STEP
    C = past_len // CMP_STRIDE
    NS = -(-(past_len + TD) // SLC_BLK)
    NSP = -(-NS // LANES) * LANES
    ov = jnp.asarray(_overlap_matrix(C, NSP))
    qw = NSA_HEADS * HEAD_DIM
    rows = PAGES_PER_STEP * page
    grid_spec = pltpu.PrefetchScalarGridSpec(
        num_scalar_prefetch=1,
        grid=(DB, npg),
        in_specs=[pl.BlockSpec((None, TD, qw), lambda b, p, pt: (b, 0, C_QA // qw)),
                  pl.BlockSpec(memory_space=pl.ANY),
                  pl.BlockSpec((None, 2, CMP_STRIDE // 2, 2 * HEAD_DIM, 2 * CMP_HID), lambda b, p, pt: (l, 0, 0, 0, 0)),
                  pl.BlockSpec((None, 2, 1, CMP_LEN * HEAD_DIM), lambda b, p, pt: (l, 0, 0, 0)),
                  pl.BlockSpec((None, 2, CMP_LEN * HEAD_DIM, CMP_HID), lambda b, p, pt: (l, 0, 0, 0)),
                  pl.BlockSpec((None, 2, 1, CMP_HID), lambda b, p, pt: (l, 0, 0, 0)),
                  pl.BlockSpec((None, 2, CMP_HID, HEAD_DIM), lambda b, p, pt: (l, 0, 0, 0)),
                  pl.BlockSpec((None, None, 1, HEAD_DIM), lambda b, p, pt: (l, 1, 0, 0)),
                  pl.BlockSpec((C, NSP), lambda b, p, pt: (0, 0))],
        out_specs=[pl.BlockSpec((None, TD, qw), lambda b, p, pt: (b, 0, 0)),
                   pl.BlockSpec((None, NSA_KV, NSA_REP * TD, NSP), lambda b, p, pt: (b, 0, 0, 0))],
        scratch_shapes=[pltpu.VMEM((2, 2 * NSA_KV, rows, HEAD_DIM), F32), pltpu.SemaphoreType.DMA((2,)),
                        pltpu.VMEM((2, NSA_KV, C, 2 * CMP_HID), F32)])
    return pl.pallas_call(
        functools.partial(_nsa_dec_select_kernel, l=l, past_len=past_len, n_blocks=NS, n_sel=min(SLC_TOPN, NS)),
        grid_spec=grid_spec,
        out_shape=[jax.ShapeDtypeStruct((DB, TD, qw), F32),
                   jax.ShapeDtypeStruct((DB, NSA_KV, NSA_REP * TD, NSP), F32)],
        compiler_params=_cparams(("arbitrary", "arbitrary")),
        name="nsa_dec_select",
    )(page_table, zs3, cache4, wp, pe, w1, b1, w2, gk, ov)


def _paged_attn_kernel(pt_ref, q_ref, sel_ref, kn_ref, vn_ref, cache_ref, o_ref, buf_ref, sem_ref,
                       m_ref, l_ref, acc_ref, *, l, j0, blk, rep):
    pg = pl.program_id(1)
    slot = _paged_fetch(cache_ref, pt_ref, l, buf_ref, sem_ref, j0)
    tk = buf_ref.shape[2]
    td = q_ref.shape[0]
    NBP = sel_ref.shape[2]
    G = sel_ref.shape[0]

    @pl.when(pg == 0)
    def _():
        _flash_reset(m_ref, l_ref, acc_ref)

    kpos = pg * tk + lax.broadcasted_iota(jnp.int32, (NBP, tk), 1)
    expand = (lax.broadcasted_iota(jnp.int32, (NBP, tk), 0) == kpos // blk).astype(BF16)
    qs = [(_stack_heads(q_ref, g, rep) * SCALE).astype(BF16) for g in range(G)]
    state = [(m_ref[g], l_ref[g], acc_ref[g]) for g in range(G)]
    new = []
    for g in range(G):
        k = buf_ref[slot, g].astype(BF16)
        v = buf_ref[slot, G + g].astype(BF16)
        s = lax.dot_general(qs[g], k, NT_DIMS, preferred_element_type=F32)
        selx = jnp.dot(sel_ref[g].astype(BF16), expand, preferred_element_type=F32)
        new.append(_online_update(state[g], s, selx > 0.5, v))
    for g in range(G):
        m_ref[g], l_ref[g], acc_ref[g] = new[g]

    @pl.when(pg == pl.num_programs(1) - 1)
    def _():
        t = lax.broadcasted_iota(jnp.int32, (rep * td, 1), 0) % td
        causal = lax.broadcasted_iota(jnp.int32, (1, td), 1) <= t
        for g in range(G):
            k = kn_ref[:, g * HEAD_DIM:(g + 1) * HEAD_DIM].astype(BF16)
            v = vn_ref[:, g * HEAD_DIM:(g + 1) * HEAD_DIM].astype(BF16)
            s = lax.dot_general(qs[g], k, NT_DIMS, preferred_element_type=F32)
            o = _online_finish(_online_update((m_ref[g], l_ref[g], acc_ref[g]), s, causal, v))
            for r in range(rep):
                h = g * rep + r
                o_ref[:, h * HEAD_DIM:(h + 1) * HEAD_DIM] = o[r * td:(r + 1) * td]


def _paged_attn(page_table, zs3, sel, cache4, l, c_q, c_k, c_v, j0, blk, name):
    DB, TD, _ = zs3.shape
    G, rows_q, NBP = sel.shape[1:]
    rep = rows_q // TD
    n_pages = page_table.shape[1]
    page = cache4.shape[2]
    npg = n_pages // PAGES_PER_STEP
    qw = G * rep * HEAD_DIM
    kw = G * HEAD_DIM
    rows = PAGES_PER_STEP * page
    grid_spec = pltpu.PrefetchScalarGridSpec(
        num_scalar_prefetch=1,
        grid=(DB, npg),
        in_specs=[pl.BlockSpec((None, TD, qw), lambda b, p, pt: (b, 0, c_q // qw)),
                  pl.BlockSpec((None, G, rows_q, NBP), lambda b, p, pt: (b, 0, 0, 0)),
                  pl.BlockSpec((None, TD, kw), lambda b, p, pt: (b, 0, c_k // kw)),
                  pl.BlockSpec((None, TD, kw), lambda b, p, pt: (b, 0, c_v // kw)),
                  pl.BlockSpec(memory_space=pl.ANY)],
        out_specs=pl.BlockSpec((None, TD, qw), lambda b, p, pt: (b, 0, 0)),
        scratch_shapes=[pltpu.VMEM((2, 2 * G, rows, HEAD_DIM), F32), pltpu.SemaphoreType.DMA((2,)),
                        pltpu.VMEM((G, rows_q, 1), F32), pltpu.VMEM((G, rows_q, 1), F32),
                        pltpu.VMEM((G, rows_q, HEAD_DIM), F32)])
    return pl.pallas_call(
        functools.partial(_paged_attn_kernel, l=l, j0=j0, blk=blk, rep=rep),
        grid_spec=grid_spec,
        out_shape=jax.ShapeDtypeStruct((DB, TD, qw), F32),
        compiler_params=_cparams(("arbitrary", "arbitrary")),
        name=name,
    )(page_table, zs3, sel, zs3, zs3, cache4)


def _nsa_dec_combine_kernel(q_ref, oc_ref, os_ref, win_ref, wn_ref, gt_ref, o_ref):
    td = q_ref.shape[0]
    wb = win_ref.shape[0]
    R = NSA_REP
    t = lax.broadcasted_iota(jnp.int32, (R * td, 1), 0) % td
    d_buf = wb + t - lax.broadcasted_iota(jnp.int32, (1, wb), 1)
    d_new = t - lax.broadcasted_iota(jnp.int32, (1, td), 1)
    for g in range(NSA_KV):
        ksl = slice(g * HEAD_DIM, (g + 1) * HEAD_DIM)
        vsl = slice((NSA_KV + g) * HEAD_DIM, (NSA_KV + g + 1) * HEAD_DIM)
        q3 = (_stack_heads(q_ref, g, R) * SCALE).astype(BF16)
        carry = _online_init(R * td)
        s = lax.dot_general(q3, win_ref[:, ksl].astype(BF16), NT_DIMS, preferred_element_type=F32)
        carry = _online_update(carry, s, (d_buf >= 0) & (d_buf < WINDOW), win_ref[:, vsl].astype(BF16))
        s = lax.dot_general(q3, wn_ref[:, ksl].astype(BF16), NT_DIMS, preferred_element_type=F32)
        carry = _online_update(carry, s, (d_new >= 0) & (d_new < WINDOW), wn_ref[:, vsl].astype(BF16))
        o_w = _online_finish(carry)
        gate = jax.nn.sigmoid(gt_ref[:, g * LANES:(g + 1) * LANES])
        for r in range(R):
            cols = slice((g * R + r) * HEAD_DIM, (g * R + r + 1) * HEAD_DIM)
            o = (gate[:, 3 * r:3 * r + 1] * oc_ref[:, cols] + gate[:, 3 * r + 1:3 * r + 2] * os_ref[:, cols]
                 + gate[:, 3 * r + 2:3 * r + 3] * o_w[r * td:(r + 1) * td])
            o_ref[:, cols] = o.astype(o_ref.dtype)


def _nsa_dec_combine(zs3, oc, o_s, win4, l):
    DB, TD, _ = zs3.shape
    qw = NSA_HEADS * HEAD_DIM
    wb, ww = win4.shape[2], win4.shape[3]
    return pl.pallas_call(
        _nsa_dec_combine_kernel,
        grid=(DB,),
        in_specs=[pl.BlockSpec((None, TD, qw), lambda b: (b, 0, C_QAR // qw)),
                  pl.BlockSpec((None, TD, qw), lambda b: (b, 0, 0)),
                  pl.BlockSpec((None, TD, qw), lambda b: (b, 0, 0)),
                  pl.BlockSpec((None, None, wb, ww), lambda b: (l, b, 0, 0)),
                  pl.BlockSpec((None, TD, ww), lambda b: (b, 0, C_WIN // ww)),
                  pl.BlockSpec((None, TD, ZT), lambda b: (b, 0, C_MISC // ZT))],
        out_specs=pl.BlockSpec((None, TD, qw), lambda b: (b, 0, 0)),
        out_shape=jax.ShapeDtypeStruct((DB, TD, qw), BF16),
        compiler_params=_cparams(("parallel",)),
        name="nsa_dec_combine",
    )(zs3, oc, o_s, win4, zs3, zs3)


def _moba_dec_select_kernel(pt_ref, q_ref, cache_ref, sel_ref, buf_ref, sem_ref, kmean_ref,
                            *, l, past_len, k_top):
    pg = pl.program_id(1)
    slot = _paged_fetch(cache_ref, pt_ref, l, buf_ref, sem_ref, 0)
    bps = buf_ref.shape[2] // MOBA_BLK
    td = q_ref.shape[0]
    NBP = kmean_ref.shape[1]
    R = MOBA_REP

    @pl.when(pg == 0)
    def _():
        kmean_ref[...] = jnp.zeros(kmean_ref.shape, F32)

    for g in range(MOBA_KV):
        x = buf_ref[slot, g]
        kmean_ref[g, pl.ds(pl.multiple_of(pg * bps, bps), bps), :] = (
            jnp.sum(x.reshape(bps, MOBA_BLK, HEAD_DIM), axis=1) * (1.0 / MOBA_BLK))

    @pl.when(pg == pl.num_programs(1) - 1)
    def _():
        own = _decode_qpos(past_len, td, R) // MOBA_BLK
        past = lax.broadcasted_iota(jnp.int32, (1, NBP), 1) < own
        for g in range(MOBA_KV):
            q3 = _stack_heads(q_ref, g, R)
            sg = lax.dot_general(q3, kmean_ref[g], NT_DIMS, precision=HI, preferred_element_type=F32)
            sg = jnp.where(past, sg, -jnp.inf)
            sel_ref[g] = ((_rank_desc(sg, past_len // MOBA_BLK, 1) < k_top) & past).astype(F32)


def _moba_dec_select(page_table, zs3, cache4, l, past_len):
    DB, TD, _ = zs3.shape
    n_pages = page_table.shape[1]
    page = cache4.shape[2]
    npg = n_pages // PAGES_PER_STEP
    NB = -(-(past_len + TD) // MOBA_BLK)
    NBP = -(-NB // LANES) * LANES
    qw = MOBA_HEADS * HEAD_DIM
    rows = PAGES_PER_STEP * page
    grid_spec = pltpu.PrefetchScalarGridSpec(
        num_scalar_prefetch=1,
        grid=(DB, npg),
        in_specs=[pl.BlockSpec((None, TD, qw), lambda b, p, pt: (b, 0, C_QB // qw)),
                  pl.BlockSpec(memory_space=pl.ANY)],
        out_specs=pl.BlockSpec((None, MOBA_KV, MOBA_REP * TD, NBP), lambda b, p, pt: (b, 0, 0, 0)),
        scratch_shapes=[pltpu.VMEM((2, MOBA_KV, rows, HEAD_DIM), F32), pltpu.SemaphoreType.DMA((2,)),
                        pltpu.VMEM((MOBA_KV, NBP, HEAD_DIM), F32)])
    return pl.pallas_call(
        functools.partial(_moba_dec_select_kernel, l=l, past_len=past_len, k_top=min(MOBA_TOPK, NB)),
        grid_spec=grid_spec,
        out_shape=jax.ShapeDtypeStruct((DB, MOBA_KV, MOBA_REP * TD, NBP), F32),
        compiler_params=_cparams(("arbitrary", "arbitrary")),
        name="moba_dec_select",
    )(page_table, zs3, cache4)


def _permute_w_in(w_in):
    sizes = (768, 256, 256, 256, 256, 256, 256, 18, 768, 256, 256, 256, 256, 512, 16, 512)
    offs = np.concatenate([[0], np.cumsum(sizes)])
    (nq, nkc, nvc, nks, nvs, nkw, nvw, ngt, mq, mk, mv, gq, gk, gv, ga, gr) = [
        w_in[:, :, int(offs[i]):int(offs[i + 1])] for i in range(len(sizes))]
    L, D = w_in.shape[0], w_in.shape[1]
    z = lambda n: jnp.zeros((L, D, n), w_in.dtype)
    misc = jnp.concatenate([ngt[:, :, :9], z(7), ga, z(96), ngt[:, :, 9:], z(119)], axis=-1)
    wz = jnp.concatenate([nq, nq, nkc, nvc, nks, nvs, nkw, nvw, mq, mk, mv, gq, gv, gr, gk, misc], axis=-1)
    assert wz.shape[-1] == NZ
    return wz.astype(BF16)


def _pair_cmp_w1(w1):
    L = w1.shape[0]
    half = CMP_STRIDE // 2
    w = w1.reshape(L, 2, CMP_LEN // CMP_STRIDE, half, 2, HEAD_DIM, CMP_HID)
    w = w.transpose(0, 1, 3, 4, 5, 2, 6)
    return w.reshape(L, 2, half, 2 * HEAD_DIM, (CMP_LEN // CMP_STRIDE) * CMP_HID).astype(BF16)


def _gain_vector(nsa_g, moba_g):
    L = nsa_g.shape[0]
    gz = jnp.ones((L, NZ), F32)
    put = lambda gz, c, g, n: gz.at[:, c:c + n * HEAD_DIM].set(jnp.tile(g, (1, n)))
    gz = put(gz, C_QA, nsa_g[:, 0], NSA_HEADS)
    gz = put(gz, C_QAR, nsa_g[:, 0], NSA_HEADS)
    gz = put(gz, C_NSA + 2 * ZT, nsa_g[:, 2], NSA_KV)
    gz = put(gz, C_WIN, nsa_g[:, 3], NSA_KV)
    gz = put(gz, C_QB, moba_g[:, 0], MOBA_HEADS)
    gz = put(gz, C_MOBA, moba_g[:, 1], MOBA_KV)
    return gz.reshape(L, 1, NZ)


def _rope_tables(pos):
    half = HEAD_DIM // 2
    inv = ROPE_THETA ** (-jnp.arange(half, dtype=F32) / half)
    ang = pos.astype(F32)[:, None] * inv
    cos, sin = jnp.cos(ang), jnp.sin(ang)
    return jnp.concatenate([cos, cos], axis=-1), jnp.concatenate([-sin, sin], axis=-1)


def kernel(x_prompt, x_sample, cache_nsa, cache_moba, state_nsa_win, state_gla, page_table, c_prompt, c_sample,
           w_ada, b_ada, norm1_g, norm2_g, w_in, nsa_qk_g, nsa_cmp_pos, nsa_cmp_w1, nsa_cmp_b1, nsa_cmp_w2,
           moba_qk_g, gla_w_a2, gla_b_a, gla_norm_g, w_out, w_up, w_down):
    B, T, D = x_prompt.shape
    DB, TD, _ = x_sample.shape
    L = w_in.shape[0]
    n_pool, page = cache_nsa.shape[1], cache_nsa.shape[2]
    past_len = page_table.shape[1] * page
    assert page_table.shape[1] % PAGES_PER_STEP == 0 and (PAGES_PER_STEP * page) % MOBA_BLK == 0
    assert T % MOBA_BLK == 0 and WINDOW % MOBA_BLK == 0
    pos_p = jnp.arange(T, dtype=jnp.int32)
    pos_s = past_len + jnp.arange(TD, dtype=jnp.int32)
    MS = DB * TD

    wz = _permute_w_in(w_in)
    gz = _gain_vector(nsa_qk_g, moba_qk_g)
    wo = w_out.astype(BF16)
    wu = w_up.astype(BF16)
    wd = w_down.astype(BF16)
    wp = _pair_cmp_w1(nsa_cmp_w1)
    n1 = norm1_g.reshape(L, 1, D)
    n2 = norm2_g.reshape(L, 1, D)
    b1 = nsa_cmp_b1.reshape(L, 2, 1, CMP_HID)
    pe_flat = nsa_cmp_pos.reshape(L, 2, 1, CMP_LEN * HEAD_DIM)
    nsa_g4 = nsa_qk_g.reshape(L, 4, 1, HEAD_DIM)
    gb = gla_b_a.reshape(L, 1, GLA_HEADS * GLA_DK)
    gng = gla_norm_g.reshape(L, 1, GLA_DV)
    cos_p, sin_p = _rope_tables(pos_p)
    cos_s, sin_s = _rope_tables(jnp.tile(pos_s, DB))
    cache_nsa4 = cache_nsa.reshape(L, n_pool, page, 4 * NSA_KV, HEAD_DIM)
    cache_moba4 = cache_moba.reshape(L, n_pool, page, 2 * MOBA_KV, HEAD_DIM)
    win4 = state_nsa_win.reshape(L, DB, state_nsa_win.shape[2], 2 * NSA_KV * HEAD_DIM)

    nc = B + DB
    c_all = jnp.concatenate([c_prompt, c_sample, jnp.zeros((-nc % 8, D), F32)], axis=0)
    mod = _ada(c_all, w_ada, b_ada)

    def mods(l):
        mp = mod[l, :B].reshape(B, 1, 6, D)
        ms = jnp.repeat(mod[l, B:nc], TD, axis=0).reshape(1, MS, 6, D)
        return [mp[:, :, i] for i in range(6)], [ms[:, :, i] for i in range(6)]

    tm_p = 1024
    tpb_p = T // tm_p
    tm_mlp = 512
    xp = x_prompt.reshape(B * T, D)
    xs = x_sample.reshape(MS, D)
    s0_p = jnp.zeros((B, GLA_HEADS, GLA_DK, GLA_DV), F32)
    outs = [[] for _ in range(8)]
    for l in range(L):
        (sh1, sc1, gt1, sh2, sc2, gt2), (sh1s, sc1s, gt1s, sh2s, sc2s, gt2s) = mods(l)
        zz = _nmm_zz(xp, sc1, sh1, n1, wz, l, gz, cos_p, sin_p, tm_p, tpb_p)
        zz3 = zz.reshape(B, T, NZ)
        kcvc = _compress_prompt(zz3, nsa_cmp_pos, nsa_cmp_w1, b1, nsa_cmp_w2, nsa_g4, l)
        oa = _nsa_prompt(zz3, kcvc)
        ob = _moba_prompt(zz3)
        og, s_fin = _gla(zz3, gla_w_a2, gb, gng, l, s0_p, 512, GLA_CHUNK)
        xp = _outproj(oa.reshape(B * T, -1), ob.reshape(B * T, -1), og.reshape(B * T, -1), wo, l, xp, gt1,
                      tm_p, tpb_p)
        xp = _mlp(xp, sc2, sh2, n2, wu, wd, l, gt2, tm_mlp, T // tm_mlp)
        outs[0].append(zz3[:, :, C_NSA:C_NSA + 4 * ZT].reshape(B, T, 4, NSA_KV, HEAD_DIM))
        outs[2].append(zz3[:, T - min(WINDOW, T):, C_WIN:C_WIN + 2 * ZT].reshape(B, -1, 2, NSA_KV, HEAD_DIM))
        outs[4].append(zz3[:, :, C_MOBA:C_MOBA + 2 * ZT].reshape(B, T, 2, MOBA_KV, HEAD_DIM))
        outs[6].append(s_fin)
        zs = _nmm_zz(xs, sc1s, sh1s, n1, wz, l, gz, cos_s, sin_s, MS, 1)
        zs3 = zs.reshape(DB, TD, NZ)
        oc_s, sel_a = _nsa_dec_select(page_table, zs3, cache_nsa4, wp, pe_flat, nsa_cmp_w1, b1, nsa_cmp_w2,
                                      nsa_g4, l, past_len)
        os_s = _paged_attn(page_table, zs3, sel_a, cache_nsa4, l, C_QAR, C_NSA + 2 * ZT, C_NSA + 3 * ZT,
                           2 * NSA_KV, SLC_BLK, "nsa_dec_selected")
        oa_s = _nsa_dec_combine(zs3, oc_s, os_s, win4, l)
        sel_b = _moba_dec_select(page_table, zs3, cache_moba4, l, past_len)
        ob_s = _paged_attn(page_table, zs3, sel_b, cache_moba4, l, C_QB, C_MOBA, C_MOBA + ZT, 0, MOBA_BLK,
                           "moba_dec_attn").astype(BF16)
        og_s, s_fin_s = _gla(zs3, gla_w_a2, gb, gng, l, state_gla[l], TD, TD)
        xs = _outproj(oa_s.reshape(MS, -1), ob_s.reshape(MS, -1), og_s.reshape(MS, -1), wo, l, xs, gt1s, MS, 1)
        xs = _mlp(xs, sc2s, sh2s, n2, wu, wd, l, gt2s, MS, 1)
        win_rows_s = zs3[:, :, C_WIN:C_WIN + 2 * ZT].reshape(DB, TD, 2, NSA_KV, HEAD_DIM)
        win_ctx = jnp.concatenate([state_nsa_win[l], win_rows_s], axis=1)
        outs[1].append(zs3[:, :, C_NSA:C_NSA + 4 * ZT].reshape(DB, TD, 4, NSA_KV, HEAD_DIM))
        outs[3].append(win_ctx[:, -min(WINDOW, past_len + TD):])
        outs[5].append(zs3[:, :, C_MOBA:C_MOBA + 2 * ZT].reshape(DB, TD, 2, MOBA_KV, HEAD_DIM))
        outs[7].append(s_fin_s)
    st = [jnp.stack(o) for o in outs]
    return (xp.reshape(B, T, D), xs.reshape(DB, TD, D), st[0], st[1], st[2], st[3], st[4], st[5], st[6], st[7])
```
